```python
import math, functools
import jax, jax.numpy as jnp
from jax import lax
import numpy as np

D_MODEL = 2048
BATCH = 4
SEQ = 2048
DEPTH = 4
DEC_BATCH = 8
DEC_SEQ = 8
PAST_LEN = 16384
PAGE_SIZE = 128

D_MIX = D_MODEL
HEAD_DIM = 64
D_ATTN = D_MIX // 2
N_HEADS = D_ATTN // HEAD_DIM
D_SSM = D_MIX // 4
SSM_GROUP = 16
N_SSM_GROUPS = D_SSM // SSM_GROUP
SSM_STATE = 64
D_CMLP = D_MIX - D_ATTN - D_SSM
CHUNK = 128
N_CMLP_GROUPS = 4
CMLP_GROUP = D_CMLP // N_CMLP_GROUPS
Q_BLOCK = 128
EPS = 1e-6
FORGET_BIAS = 4.0
_SPLITS = (D_ATTN, D_ATTN, D_ATTN, N_HEADS, D_ATTN, D_SSM, D_SSM, D_CMLP, D_CMLP, D_CMLP)
D_IN_PROJ = 4 * D_ATTN + N_HEADS + 2 * D_SSM + 3 * D_CMLP

kernel_name = 'hybrid_fox_s5_sgu_decoder_step'

F32 = jnp.float32


def rms_norm(x, g):
    xf = x.astype(F32)
    y = xf * lax.rsqrt(jnp.mean(xf * xf, axis=-1, keepdims=True) + EPS)
    return (y * g.astype(F32)).astype(x.dtype)


def ada_mod(c, w_ada, b_ada):
    m = jax.nn.silu(c) @ w_ada + b_ada
    shift, scale, gate = jnp.split(m, 3, axis=-1)
    return shift[:, None], scale[:, None], gate[:, None]


def split_proj(z):
    cuts = []
    acc = 0
    for w in _SPLITS[:-1]:
        acc += w
        cuts.append(acc)
    return jnp.split(z, cuts, axis=-1)


def fox_prompt(q, k, v, logf):
    bsz, s_len, h, dh = q.shape
    scale = HEAD_DIM ** -0.5
    F = jnp.cumsum(logf, axis=1)
    F_key = F.transpose(0, 2, 1)
    kf = k.astype(F32)
    vf = v.astype(F32)
    key_pos = jnp.arange(s_len)

    def block(i):
        start = i * Q_BLOCK
        qb = lax.dynamic_slice_in_dim(q, start, Q_BLOCK, axis=1).astype(F32)
        Fb = lax.dynamic_slice_in_dim(F, start, Q_BLOCK, axis=1).transpose(0, 2, 1)
        s = jnp.einsum('bqhd,bkhd->bhqk', qb, kf) * scale + Fb[..., None] - F_key[:, :, None, :]
        q_pos = start + jnp.arange(Q_BLOCK)
        s = jnp.where(key_pos[None, :] <= q_pos[:, None], s, -jnp.inf)
        p = jax.nn.softmax(s, axis=-1)
        return jnp.einsum('bhqk,bkhd->bqhd', p, vf)

    out = lax.map(block, jnp.arange(s_len // Q_BLOCK))
    return out.transpose(1, 0, 2, 3, 4).reshape(bsz, s_len, h * dh)


def fox_sample(q, k, v, logf, k_past, v_past, logf_past):
    bsz, t_len, h, dh = q.shape
    p_len = k_past.shape[1]
    scale = HEAD_DIM ** -0.5
    qf = q.astype(F32)
    cn = jnp.cumsum(logf, axis=1).transpose(0, 2, 1)
    lfp = logf_past.astype(F32)
    srev = (lax.cumsum(lfp, axis=1, reverse=True) - lfp).transpose(0, 2, 1)
    s_past = jnp.einsum('bqhd,bkhd->bhqk', qf, k_past.astype(F32)) * scale + srev[:, :, None, :]
    s_new = jnp.einsum('bqhd,bkhd->bhqk', qf, k.astype(F32)) * scale - cn[:, :, None, :]
    causal = jnp.tril(jnp.ones((t_len, t_len), dtype=bool))
    s_new = jnp.where(causal, s_new, -jnp.inf)
    s = jnp.concatenate([s_past, s_new], axis=-1) + cn[..., None]
    p = jax.nn.softmax(s, axis=-1)
    out = (jnp.einsum('bhqk,bkhd->bqhd', p[..., :p_len], v_past.astype(F32))
           + jnp.einsum('bhqk,bkhd->bqhd', p[..., p_len:], v.astype(F32)))
    return out.reshape(bsz, t_len, h * dh)


def _cmul(ar, ai, br, bi):
    return ar * br - ai * bi, ar * bi + ai * br


def _scan_op(e1, e2):
    a1r, a1i, b1r, b1i = e1
    a2r, a2i, b2r, b2i = e2
    ar, ai = _cmul(a2r, a2i, a1r, a1i)
    br, bi = _cmul(a2r, a2i, b1r, b1i)
    return ar, ai, br + b2r, bi + b2i


def s5_branch(xs, h0r, h0i, lam_re, lam_im, log_dt, b_re, b_im, c_re, c_im, d_skip, w_glu, b_glu):
    bsz, t_len, _ = xs.shape
    xf = xs.astype(F32)
    xg = xf.reshape(bsz, t_len, N_SSM_GROUPS, SSM_GROUP)
    lr = lam_re.astype(F32)
    li = lam_im.astype(F32)
    dt = jnp.exp(log_dt.astype(F32))[:, None]
    mag = jnp.exp(lr * dt)
    abr = mag * jnp.cos(li * dt)
    abi = mag * jnp.sin(li * dt)
    den = lr * lr + li * li
    pr = abr - 1.0
    fr = (pr * lr + abi * li) / den
    fi = (abi * lr - pr * li) / den
    bbr, bbi = _cmul(fr[..., None], fi[..., None], b_re.astype(F32), b_im.astype(F32))
    ur = jnp.einsum('gnc,btgc->btgn', bbr, xg)
    ui = jnp.einsum('gnc,btgc->btgn', bbi, xg)
    ar = jnp.broadcast_to(abr, ur.shape)
    ai = jnp.broadcast_to(abi, ui.shape)
    Ar, Ai, Hr, Hi = lax.associative_scan(_scan_op, (ar, ai, ur, ui), axis=1)
    t1r, t1i = _cmul(Ar, Ai, h0r.astype(F32)[:, None], h0i.astype(F32)[:, None])
    hr = Hr + t1r
    hi = Hi + t1i
    y = (jnp.einsum('gcn,btgn->btgc', c_re.astype(F32), hr)
         - jnp.einsum('gcn,btgn->btgc', c_im.astype(F32), hi))
    y = y.reshape(bsz, t_len, D_SSM) + d_skip.astype(F32) * xf
    y = jax.nn.gelu(y)
    y = y * jax.nn.sigmoid(y @ w_glu.astype(F32) + b_glu.astype(F32))
    return y, hr[:, -1], hi[:, -1]


def chunk_mlp_branch(u, v, ln_g, ln_b, w_s, b_s):
    bsz, t_len, _ = u.shape
    vf = v.astype(F32)
    mu = jnp.mean(vf, axis=-1, keepdims=True)
    var = jnp.mean(jnp.square(vf - mu), axis=-1, keepdims=True)
    vn = (vf - mu) * lax.rsqrt(var + EPS) * ln_g.astype(F32) + ln_b.astype(F32)
    L = min(t_len, CHUNK)
    vc = vn.reshape(bsz, t_len // L, L, N_CMLP_GROUPS, CMLP_GROUP)
    w = w_s[:, :L, :L].astype(F32) * jnp.tril(jnp.ones((L, L), F32))
    bias = b_s[:, :L].astype(F32).T[None, None, :, :, None]
    z = jnp.einsum('gts,bcsgd->bctgd', w, vc) + bias
    out = u.astype(F32) * z.reshape(bsz, t_len, D_CMLP)
    return out, vn


def trunk_layer(x, c, attn_fn, h0r, h0i, norm_g, w_ada, b_ada, w_in, b_f, q_norm_g, k_norm_g,
                lam_re, lam_im, log_dt, b_re, b_im, c_re, c_im, d_skip, w_glu, b_glu,
                sgu_ln_g, sgu_ln_b, w_s, b_s, w_out):
    bsz, t_len, _ = x.shape
    shift, scale, gate = ada_mod(c, w_ada, b_ada)
    h = rms_norm(x, norm_g) * (1.0 + scale) + shift
    z = h @ w_in
    q, k, v, fg, ga, xs, gs, u, vv, gm = split_proj(z)
    qh = rms_norm(q.reshape(bsz, t_len, N_HEADS, HEAD_DIM), q_norm_g)
    kh = rms_norm(k.reshape(bsz, t_len, N_HEADS, HEAD_DIM), k_norm_g)
    vh = v.reshape(bsz, t_len, N_HEADS, HEAD_DIM)
    logf = jax.nn.log_sigmoid((fg + b_f).astype(F32))
    a_out = attn_fn(qh, kh, vh, logf) * jax.nn.silu(ga.astype(F32))
    s_out, hr, hi = s5_branch(xs, h0r, h0i, lam_re, lam_im, log_dt, b_re, b_im, c_re, c_im,
                              d_skip, w_glu, b_glu)
    s_out = s_out * jax.nn.silu(gs.astype(F32))
    m_out, vn = chunk_mlp_branch(u, vv, sgu_ln_g, sgu_ln_b, w_s, b_s)
    m_out = m_out * jax.nn.silu(gm.astype(F32))
    mix = jnp.concatenate([a_out, s_out, m_out], axis=-1).astype(x.dtype)
    y = x + gate * (mix @ w_out)
    return y.astype(x.dtype), (kh, vh, logf, hr, hi, vn)


def setup_inputs(seed: int = 0) -> dict:
    key = jax.random.key(seed)
    ks = jax.random.split(key, 40)
    n_pages = PAST_LEN // PAGE_SIZE
    n_used = DEC_BATCH * n_pages
    n_pool = n_used + max(n_used // 4, 1)
    G, N, C = N_SSM_GROUPS, SSM_STATE, SSM_GROUP

    def nrm(k, shape, s=1.0):
        return s * jax.random.normal(k, shape, F32)

    perm = jax.random.permutation(ks[0], n_pool)
    page_table = perm[:n_used].reshape(DEC_BATCH, n_pages).astype(jnp.int32)
    return {
        'x_prompt': nrm(ks[5], (BATCH, SEQ, D_MODEL)),
        'x_sample': nrm(ks[6], (DEC_BATCH, DEC_SEQ, D_MODEL)),
        'c_prompt': nrm(ks[7], (BATCH, D_MODEL)),
        'c_sample': nrm(ks[8], (DEC_BATCH, D_MODEL)),
        'cache_k': nrm(ks[1], (n_pool, DEPTH, PAGE_SIZE, N_HEADS, HEAD_DIM)),
        'cache_v': nrm(ks[2], (n_pool, DEPTH, PAGE_SIZE, N_HEADS, HEAD_DIM)),
        'cache_logf': jax.nn.log_sigmoid(FORGET_BIAS + nrm(ks[3], (n_pool, DEPTH, PAGE_SIZE, N_HEADS))),
        'state_ssm_re': nrm(ks[4], (DEC_BATCH, DEPTH, G, N), 0.3),
        'state_ssm_im': nrm(ks[31], (DEC_BATCH, DEPTH, G, N), 0.3),
        'page_table': page_table,
        'norm_g': 1.0 + nrm(ks[9], (DEPTH, D_MODEL), 0.05),
        'w_ada': nrm(ks[10], (DEPTH, D_MODEL, 3 * D_MODEL), 0.5 * D_MODEL ** -0.5),
        'b_ada': nrm(ks[11], (DEPTH, 3 * D_MODEL), 0.02),
        'w_in': nrm(ks[12], (DEPTH, D_MODEL, D_IN_PROJ), D_MODEL ** -0.5),
        'b_f': FORGET_BIAS + nrm(ks[13], (DEPTH, N_HEADS), 0.5),
        'q_norm_g': 1.0 + nrm(ks[14], (DEPTH, HEAD_DIM), 0.05),
        'k_norm_g': 1.0 + nrm(ks[15], (DEPTH, HEAD_DIM), 0.05),
        'lam_re': -0.5 + nrm(ks[16], (DEPTH, G, N), 0.01),
        'lam_im': math.pi * jnp.arange(N, dtype=F32) + nrm(ks[17], (DEPTH, G, N), 0.01),
        'log_dt': jax.random.uniform(ks[18], (DEPTH, G), F32, math.log(1e-3), math.log(1e-1)),
        'b_re': nrm(ks[19], (DEPTH, G, N, C), (2 * C) ** -0.5),
        'b_im': nrm(ks[20], (DEPTH, G, N, C), (2 * C) ** -0.5),
        'c_re': nrm(ks[21], (DEPTH, G, C, N), N ** -0.5),
        'c_im': nrm(ks[22], (DEPTH, G, C, N), N ** -0.5),
        'd_skip': nrm(ks[23], (DEPTH, D_SSM)),
        'w_glu': nrm(ks[24], (DEPTH, D_SSM, D_SSM), D_SSM ** -0.5),
        'b_glu': nrm(ks[25], (DEPTH, D_SSM), 0.02),
        'sgu_ln_g': 1.0 + nrm(ks[26], (DEPTH, D_CMLP), 0.05),
        'sgu_ln_b': nrm(ks[27], (DEPTH, D_CMLP), 0.02),
        'w_s': nrm(ks[28], (DEPTH, N_CMLP_GROUPS, CHUNK, CHUNK), CHUNK ** -0.5),
        'b_s': 1.0 + nrm(ks[29], (DEPTH, N_CMLP_GROUPS, CHUNK), 0.1),
        'w_out': nrm(ks[30], (DEPTH, D_MIX, D_MODEL), D_MIX ** -0.5),
    }


def reference(x_prompt, x_sample, c_prompt, c_sample, cache_k, cache_v, cache_logf,
              state_ssm_re, state_ssm_im, page_table, norm_g, w_ada, b_ada, w_in, b_f,
              q_norm_g, k_norm_g, lam_re, lam_im, log_dt, b_re, b_im, c_re, c_im, d_skip,
              w_glu, b_glu, sgu_ln_g, sgu_ln_b, w_s, b_s, w_out):
    n_dec, n_pages = page_table.shape
    past = n_pages * PAGE_SIZE
    bp = x_prompt.shape[0]
    h_zero = jnp.zeros((bp, N_SSM_GROUPS, SSM_STATE), F32)
    yp = x_prompt
    ys = x_sample
    kp_l, vp_l, lfp_l, hrp_l, hip_l = [], [], [], [], []
    ks_l, vs_l, lfs_l, hrs_l, his_l, vns_l = [], [], [], [], [], []
    for l in range(DEPTH):
        params = (norm_g[l], w_ada[l], b_ada[l], w_in[l], b_f[l], q_norm_g[l], k_norm_g[l],
                  lam_re[l], lam_im[l], log_dt[l], b_re[l], b_im[l], c_re[l], c_im[l],
                  d_skip[l], w_glu[l], b_glu[l], sgu_ln_g[l], sgu_ln_b[l], w_s[l], b_s[l],
                  w_out[l])
        yp, (kp, vp, lfp, hrp, hip, _) = trunk_layer(yp, c_prompt, fox_prompt, h_zero, h_zero, *params)
        kp_l.append(kp); vp_l.append(vp); lfp_l.append(lfp); hrp_l.append(hrp); hip_l.append(hip)
        k_past = cache_k[page_table, l].reshape(n_dec, past, N_HEADS, HEAD_DIM)
        v_past = cache_v[page_table, l].reshape(n_dec, past, N_HEADS, HEAD_DIM)
        lf_past = cache_logf[page_table, l].reshape(n_dec, past, N_HEADS)
        attn_s = functools.partial(fox_sample, k_past=k_past, v_past=v_past, logf_past=lf_past)
        ys, (ksn, vsn, lfs, hrs, his, vns) = trunk_layer(ys, c_sample, attn_s, state_ssm_re[:, l],
                                                         state_ssm_im[:, l], *params)
        ks_l.append(ksn); vs_l.append(vsn); lfs_l.append(lfs); hrs_l.append(hrs); his_l.append(his)
        vns_l.append(vns)
    k_prompt = jnp.stack(kp_l, axis=1)
    v_prompt = jnp.stack(vp_l, axis=1)
    logf_prompt = jnp.stack(lfp_l, axis=1)
    ssm_re_prompt = jnp.stack(hrp_l, axis=1)
    ssm_im_prompt = jnp.stack(hip_l, axis=1)
    k_sample = jnp.stack(ks_l, axis=1)
    v_sample = jnp.stack(vs_l, axis=1)
    logf_sample = jnp.stack(lfs_l, axis=1)
    ssm_re_sample = jnp.stack(hrs_l, axis=1)
    ssm_im_sample = jnp.stack(his_l, axis=1)
    cmlp_v_sample = jnp.stack(vns_l, axis=1)
    return (yp, ys, k_prompt, v_prompt, logf_prompt, ssm_re_prompt, ssm_im_prompt,
            k_sample, v_sample, logf_sample, ssm_re_sample, ssm_im_sample, cmlp_v_sample)
```

```python
import functools

import jax
import jax.numpy as jnp
from jax import lax
from jax.experimental import pallas as pl
from jax.experimental.pallas import tpu as pltpu

F32 = jnp.float32
BF16 = jnp.bfloat16

D_MODEL = 2048
HEAD_DIM = 64
D_ATTN = 1024
N_HEADS = 16
D_SSM = 512
SSM_GROUP = 16
N_SSM_GROUPS = 32
SSM_STATE = 64
D_CMLP = 512
CHUNK = 128
N_CMLP_GROUPS = 4
CMLP_GROUP = 128
PAGE_SIZE = 128
EPS = 1e-6
LANES = 128
SUBLANES = 8
NEG_BIG = -1e30

COL_Q, COL_K, COL_V, COL_GA = 0, 1024, 2048, 3072
COL_XS, COL_GS, COL_U, COL_VV, COL_GM = 4096, 4608, 5120, 5632, 6144
D_Z = 6656
D_STATE = 2 * N_SSM_GROUPS * SSM_STATE
S5_SETS = 2
SET_X = D_SSM // S5_SETS
SET_H = D_STATE // S5_SETS
PAGES_PER_STEP = 8


def _cparams(sem, vmem_mb):
    return pltpu.CompilerParams(dimension_semantics=sem, vmem_limit_bytes=vmem_mb * 1024 * 1024)


def _split3(x):
    hi = x.astype(BF16)
    r1 = x - hi.astype(F32)
    mid = r1.astype(BF16)
    lo = (r1 - mid.astype(F32)).astype(BF16)
    return hi, mid, lo


def _dot(a, b):
    return jnp.dot(a, b, preferred_element_type=F32)


def _dot_nt(a, b):
    return lax.dot_general(a, b, (((1,), (1,)), ((), ())), preferred_element_type=F32)


def _silu(x):
    return x * jax.nn.sigmoid(x)


def _ada_kernel(c_ref, w_ref, b_ref, o_ref):
    s = _silu(c_ref[...]).astype(BF16)
    o_ref[...] = _dot(s, w_ref[...].astype(BF16)) + b_ref[...]


def _ada_all(c_all, w_ada, b_ada):
    depth = w_ada.shape[0]
    rows = c_all.shape[0]
    tn = 1024
    return pl.pallas_call(
        _ada_kernel,
        grid=(depth, 3 * D_MODEL // tn),
        in_specs=[
            pl.BlockSpec((rows, D_MODEL), lambda l, j: (0, 0)),
            pl.BlockSpec((None, D_MODEL, tn), lambda l, j: (l, 0, j)),
            pl.BlockSpec((None, 1, tn), lambda l, j: (l, 0, j)),
        ],
        out_specs=pl.BlockSpec((None, rows, tn), lambda l, j: (l, 0, j)),
        out_shape=jax.ShapeDtypeStruct((depth, rows, 3 * D_MODEL), F32),
        compiler_params=_cparams(("arbitrary", "arbitrary"), 48),
        name="ada",
    )(c_all, w_ada, b_ada.reshape(depth, 1, 3 * D_MODEL))


def _inproj_kernel(x_ref, g_ref, sc_ref, sh_ref, w_ref, wfg_ref, bf_ref, z_ref, lf_ref, h_ref):
    @pl.when(pl.program_id(1) == 0)
    def _():
        x = x_ref[...]
        ms = jnp.mean(x * x, axis=-1, keepdims=True)
        y = x * lax.rsqrt(ms + EPS) * g_ref[...]
        hb = (y * (1.0 + sc_ref[...]) + sh_ref[...]).astype(BF16)
        h_ref[...] = hb
        fg = _dot(hb, wfg_ref[...]) + bf_ref[...]
        lf_ref[...] = jnp.minimum(fg, 0.0) - jnp.log1p(jnp.exp(-jnp.abs(fg)))

    z_ref[...] = _dot(h_ref[...], w_ref[...])


def _inproj(x, norm_g, scale, shift, w_main, w_fg, b_f, tm, rows_per_mod):
    t = x.shape[0]
    tn = 512
    mod_rows = scale.shape[1]
    mod_spec = pl.BlockSpec((None, mod_rows, D_MODEL), lambda i, j: ((i * tm) // rows_per_mod, 0, 0))
    return pl.pallas_call(
        _inproj_kernel,
        grid=(t // tm, D_Z // tn),
        in_specs=[
            pl.BlockSpec((tm, D_MODEL), lambda i, j: (i, 0)),
            pl.BlockSpec((1, D_MODEL), lambda i, j: (0, 0)),
            mod_spec,
            mod_spec,
            pl.BlockSpec((D_MODEL, tn), lambda i, j: (0, j)),
            pl.BlockSpec((D_MODEL, LANES), lambda i, j: (0, 0)),
            pl.BlockSpec((1, LANES), lambda i, j: (0, 0)),
        ],
        out_specs=[
            pl.BlockSpec((tm, tn), lambda i, j: (i, j)),
            pl.BlockSpec((tm, LANES), lambda i, j: (i, 0)),
        ],
        out_shape=[jax.ShapeDtypeStruct((t, D_Z), F32), jax.ShapeDtypeStruct((t, LANES), F32)],
        scratch_shapes=[pltpu.VMEM((tm, D_MODEL), BF16)],
        compiler_params=_cparams(("arbitrary", "arbitrary"), 48),
        name="inproj",
    )(x, norm_g, scale, shift, w_main, w_fg, b_f)


def _qkprep_kernel(seq_len, q_ref, k_ref, v_ref, lf_ref, qg_ref, kg_ref, seg_ref, pq_ref, pk_ref,
                   qa_ref, ka_ref, kn_ref, vb_ref, f_ref, carry_ref):
    tm = q_ref.shape[0]
    row = lax.broadcasted_iota(jnp.int32, (tm, tm), 0)
    col = lax.broadcasted_iota(jnp.int32, (tm, tm), 1)
    tri = col <= row
    if seq_len < tm:
        shift = seq_len.bit_length() - 1
        tri = tri & (lax.shift_right_logical(row, shift) == lax.shift_right_logical(col, shift))
    tri_b = jnp.where(tri, 1.0, 0.0).astype(BF16)
    hi, mid, lo = _split3(lf_ref[...])
    f = _dot(tri_b, hi) + _dot(tri_b, mid) + _dot(tri_b, lo)
    if seq_len > tm:
        @pl.when(pl.program_id(0) % (seq_len // tm) == 0)
        def _():
            carry_ref[...] = jnp.zeros_like(carry_ref)

        f = f + carry_ref[...]
        carry_ref[...] = f[tm - 1:tm, :]
    f_ref[...] = f

    def head_norm(x, g):
        ss = _dot((x * x).astype(BF16), seg_ref[...])
        return x * lax.rsqrt(ss * (1.0 / HEAD_DIM) + EPS) * g

    qs = head_norm(q_ref[...], qg_ref[...]) * (HEAD_DIM ** -0.5)
    kn = head_norm(k_ref[...], kg_ref[...])
    kn_ref[...] = kn
    vb_ref[...] = v_ref[...].astype(BF16)

    fh, fm, fl = _split3(f)
    lane = lax.broadcasted_iota(jnp.int32, (tm, LANES), 1)
    parts = jnp.where(lane < 16, fh.astype(F32),
                      jnp.where(lane < 32, pltpu.roll(fm.astype(F32), 16, 1),
                                jnp.where(lane < 48, pltpu.roll(fl.astype(F32), 32, 1),
                                          jnp.where(lane == 48, 1.0, 0.0)))).astype(BF16)
    ext_q = _dot(parts, pq_ref[...])
    ext_k = _dot(parts, pk_ref[...])
    low = lane < HEAD_DIM
    for src, ext, dst in ((qs, ext_q, qa_ref), (kn, ext_k, ka_ref)):
        for p in range(N_HEADS // 2):
            blk = src[:, p * LANES:(p + 1) * LANES]
            e = ext[:, p * LANES:(p + 1) * LANES]
            dst[:, (2 * p) * LANES:(2 * p + 1) * LANES] = jnp.where(low, blk, pltpu.roll(e, 64, 1)).astype(BF16)
            dst[:, (2 * p + 1) * LANES:(2 * p + 2) * LANES] = jnp.where(low, pltpu.roll(blk, 64, 1), e).astype(BF16)


def _aug_placement():
    r = jnp.arange(LANES)[:, None]
    c = jnp.arange(N_HEADS * HEAD_DIM)[None, :]
    rp, rh = r // 16, r % 16
    ch, cc = c // HEAD_DIM, c % HEAD_DIM
    f_rows = (r < 48) & (rh == ch)
    one_row = r == 48
    pq = jnp.where(f_rows & (cc == rp), 1.0, 0.0) + jnp.where(one_row & (cc >= 3) & (cc < 6), 1.0, 0.0)
    pk = jnp.where(f_rows & (cc == rp + 3), -1.0, 0.0) + jnp.where(one_row & (cc < 3), 1.0, 0.0)
    seg = jnp.where(jnp.arange(D_ATTN)[:, None] // HEAD_DIM == jnp.arange(D_ATTN)[None, :] // HEAD_DIM, 1.0, 0.0)
    return pq.astype(BF16), pk.astype(BF16), seg.astype(BF16)


def _qkprep(z, logf, qg, kg, consts, tm, seq_len):
    t = z.shape[0]
    pq, pk, seg = consts
    col = lambda c: pl.BlockSpec((tm, D_ATTN), lambda i: (i, c))
    const = lambda shape: pl.BlockSpec(shape, lambda i: (0, 0))
    return pl.pallas_call(
        functools.partial(_qkprep_kernel, seq_len),
        grid=(t // tm,),
        in_specs=[col(COL_Q // D_ATTN), col(COL_K // D_ATTN), col(COL_V // D_ATTN),
                  pl.BlockSpec((tm, LANES), lambda i: (i, 0)),
                  const((1, D_ATTN)), const((1, D_ATTN)), const((D_ATTN, D_ATTN)),
                  const((LANES, D_ATTN)), const((LANES, D_ATTN))],
        out_specs=[pl.BlockSpec((tm, 2 * D_ATTN), lambda i: (i, 0)),
                   pl.BlockSpec((tm, 2 * D_ATTN), lambda i: (i, 0)),
                   pl.BlockSpec((tm, D_ATTN), lambda i: (i, 0)),
                   pl.BlockSpec((tm, D_ATTN), lambda i: (i, 0)),
                   pl.BlockSpec((tm, LANES), lambda i: (i, 0))],
        out_shape=[jax.ShapeDtypeStruct((t, 2 * D_ATTN), BF16), jax.ShapeDtypeStruct((t, 2 * D_ATTN), BF16),
                   jax.ShapeDtypeStruct((t, D_ATTN), F32), jax.ShapeDtypeStruct((t, D_ATTN), BF16),
                   jax.ShapeDtypeStruct((t, LANES), F32)],
        scratch_shapes=[pltpu.VMEM((1, LANES), F32)],
        compiler_params=_cparams(("arbitrary",), 48),
        name="qkprep",
    )(z, z, z, logf, qg, kg, seg, pq, pk)


def _fox_kernel(qa_ref, ka_ref, v_ref, ga_ref, o_ref, m_ref, l_ref, acc_ref):
    i, j = pl.program_id(2), pl.program_id(3)
    tq, tk = qa_ref.shape[0], ka_ref.shape[0]

    @pl.when(j == 0)
    def _():
        m_ref[...] = jnp.full_like(m_ref, NEG_BIG)
        l_ref[...] = jnp.zeros_like(l_ref)
        acc_ref[...] = jnp.zeros_like(acc_ref)

    def update(masked):
        v = v_ref[...]
        for hh in range(2):
            s = _dot_nt(qa_ref[:, hh * LANES:(hh + 1) * LANES], ka_ref[:, hh * LANES:(hh + 1) * LANES])
            if masked:
                row = lax.broadcasted_iota(jnp.int32, (tq, tk), 0)
                col = lax.broadcasted_iota(jnp.int32, (tq, tk), 1)
                s = jnp.where(col <= row, s, NEG_BIG)
            m_prev = m_ref[hh]
            m_new = jnp.maximum(m_prev, jnp.max(s, axis=1, keepdims=True))
            alpha = jnp.exp(m_prev - m_new)
            p = jnp.exp(s - m_new)
            l_ref[hh] = alpha * l_ref[hh] + jnp.sum(p, axis=1, keepdims=True)
            acc_ref[hh] = alpha * acc_ref[hh] + _dot(p.astype(BF16), v)
            m_ref[hh] = m_new

    @pl.when(j < i)
    def _():
        update(False)

    @pl.when(j == i)
    def _():
        update(True)
        lane = lax.broadcasted_iota(jnp.int32, (tq, LANES), 1)
        o = jnp.where(lane < HEAD_DIM, acc_ref[0] / l_ref[0], acc_ref[1] / l_ref[1])
        o_ref[...] = (o * _silu(ga_ref[...])).astype(BF16)


def _fox_prompt(qa, ka, vb, z, batch, seq, tq):
    nq = seq // tq
    pairs = N_HEADS // 2
    ga_blk = COL_GA // LANES
    return pl.pallas_call(
        _fox_kernel,
        grid=(batch, pairs, nq, nq),
        in_specs=[
            pl.BlockSpec((tq, 2 * LANES), lambda b, p, i, j: (b * nq + i, p)),
            pl.BlockSpec((tq, 2 * LANES), lambda b, p, i, j: (b * nq + jnp.minimum(i, j), p)),
            pl.BlockSpec((tq, LANES), lambda b, p, i, j: (b * nq + jnp.minimum(i, j), p)),
            pl.BlockSpec((tq, LANES), lambda b, p, i, j: (b * nq + i, ga_blk + p)),
        ],
        out_specs=pl.BlockSpec((tq, LANES), lambda b, p, i, j: (b * nq + i, p)),
        out_shape=jax.ShapeDtypeStruct((batch * seq, D_ATTN), BF16),
        scratch_shapes=[pltpu.VMEM((2, tq, 1), F32), pltpu.VMEM((2, tq, 1), F32),
                        pltpu.VMEM((2, tq, LANES), F32)],
        compiler_params=_cparams(("arbitrary",) * 4, 48),
        name="fox",
    )(qa, ka, vb, z)


def _srev_kernel(pt_ref, *refs):
    lf_refs = refs[:PAGES_PER_STEP]
    out_ref, pad_ref, carry_ref = refs[PAGES_PER_STEP:]

    @pl.when(pl.program_id(2) == 0)
    def _():
        carry_ref[...] = jnp.zeros_like(carry_ref)
        pad_ref[...] = jnp.zeros_like(pad_ref)

    row = lax.broadcasted_iota(jnp.int32, (PAGE_SIZE, PAGE_SIZE), 0)
    col = lax.broadcasted_iota(jnp.int32, (PAGE_SIZE, PAGE_SIZE), 1)
    later = jnp.where(col > row, 1.0, 0.0).astype(BF16)
    for i in reversed(range(PAGES_PER_STEP)):
        pad_ref[:, 0:N_HEADS] = lf_refs[i][...]
        x = pad_ref[...]
        hi, mid, lo = _split3(x)
        srev = _dot(later, hi) + _dot(later, mid) + _dot(later, lo) + carry_ref[...]
        carry_ref[...] = carry_ref[...] + jnp.sum(x, axis=0, keepdims=True)
        out_ref[:, i * PAGE_SIZE:(i + 1) * PAGE_SIZE] = srev.T[0:N_HEADS, :]


def _srev_all(cache_logf, page_table):
    depth = cache_logf.shape[1]
    n_dec, n_pages = page_table.shape
    nchunk = n_pages // PAGES_PER_STEP

    def page_spec(i):
        return pl.BlockSpec(
            (None, None, PAGE_SIZE, N_HEADS),
            lambda l, b, c, pt: (pt[b, (nchunk - 1 - c) * PAGES_PER_STEP + i], l, 0, 0))

    grid_spec = pltpu.PrefetchScalarGridSpec(
        num_scalar_prefetch=1,
        grid=(depth, n_dec, nchunk),
        in_specs=[page_spec(i) for i in range(PAGES_PER_STEP)],
        out_specs=pl.BlockSpec((None, None, N_HEADS, PAGES_PER_STEP * PAGE_SIZE),
                               lambda l, b, c, pt: (l, b, 0, nchunk - 1 - c)),
        scratch_shapes=[pltpu.VMEM((PAGE_SIZE, LANES), F32), pltpu.VMEM((1, LANES), F32)],
    )
    return pl.pallas_call(
        _srev_kernel,
        grid_spec=grid_spec,
        out_shape=jax.ShapeDtypeStruct((depth, n_dec, N_HEADS, n_pages * PAGE_SIZE), F32),
        compiler_params=_cparams(("arbitrary",) * 3, 32),
        name="srev",
    )(page_table, *([cache_logf] * PAGES_PER_STEP))


def _foxdec_kernel(pt_ref, *refs):
    n = PAGES_PER_STEP
    q_ref = refs[0]
    k_refs = refs[1:1 + n]
    v_refs = refs[1 + n:1 + 2 * n]
    srev_ref, knew_ref, vnew_ref, cncol_ref, cnrow_ref, ga_ref, o_ref, m_ref, l_ref, acc_ref = refs[1 + 2 * n:]
    c = pl.program_id(1)
    rows = q_ref.shape[0]

    @pl.when(c == 0)
    def _():
        m_ref[...] = jnp.full_like(m_ref, NEG_BIG)
        l_ref[...] = jnp.zeros_like(l_ref)
        acc_ref[...] = jnp.zeros_like(acc_ref)

    def accumulate(s, v):
        m_prev = m_ref[...]
        m_new = jnp.maximum(m_prev, jnp.max(s, axis=1, keepdims=True))
        alpha = jnp.exp(m_prev - m_new)
        p = jnp.exp(s - m_new)
        l_ref[...] = alpha * l_ref[...] + jnp.sum(p, axis=1, keepdims=True)
        acc_ref[...] = alpha * acc_ref[...] + _dot(p.astype(BF16), v)
        m_ref[...] = m_new

    q = q_ref[...]
    cn_col = cncol_ref[...]
    kc = jnp.concatenate([r[...].astype(BF16) for r in k_refs], axis=0)
    vc = jnp.concatenate([r[...].astype(BF16) for r in v_refs], axis=0)
    npos = n * PAGE_SIZE
    s = _dot_nt(q, kc)
    s = (s.reshape(rows // N_HEADS, N_HEADS, npos) + srev_ref[...][None]).reshape(rows, npos)
    s = s + jnp.concatenate([cn_col] * n, axis=1)
    accumulate(s, vc)

    @pl.when(c == pl.num_programs(1) - 1)
    def _():
        sn = _dot_nt(q, knew_ref[...]) - cnrow_ref[...] + cn_col
        row = lax.broadcasted_iota(jnp.int32, (rows, LANES), 0)
        col = lax.broadcasted_iota(jnp.int32, (rows, LANES), 1)
        sn = jnp.where(col <= lax.shift_right_logical(row, 4), sn, NEG_BIG)
        accumulate(sn, vnew_ref[...])
        o = acc_ref[...] / l_ref[...]
        row = lax.broadcasted_iota(jnp.int32, (rows, D_ATTN), 0)
        col = lax.broadcasted_iota(jnp.int32, (rows, D_ATTN), 1)
        own = (row & (N_HEADS - 1)) == lax.shift_right_logical(col, 6)
        o = jnp.sum(jnp.where(own, o, 0.0).reshape(rows // N_HEADS, N_HEADS, D_ATTN), axis=1)
        o_ref[...] = o * _silu(ga_ref[...])


def _fox_sample(layer, page_table, qbd, cache_k, cache_v, srev, knew, vnew, cn_col, cn_row, z):
    n_dec, n_pages = page_table.shape
    nchunk = n_pages // PAGES_PER_STEP
    rows = qbd.shape[1]
    t_len = rows // N_HEADS

    def page_spec(i):
        return pl.BlockSpec((None, None, PAGE_SIZE, D_ATTN),
                            lambda b, c, pt: (pt[b, c * PAGES_PER_STEP + i], layer, 0, 0))

    per_b = lambda shape: pl.BlockSpec((None,) + shape, lambda b, c, pt: (b, 0, 0))
    grid_spec = pltpu.PrefetchScalarGridSpec(
        num_scalar_prefetch=1,
        grid=(n_dec, nchunk),
        in_specs=([per_b((rows, D_ATTN))]
                  + [page_spec(i) for i in range(PAGES_PER_STEP)]
                  + [page_spec(i) for i in range(PAGES_PER_STEP)]
                  + [pl.BlockSpec((None, None, N_HEADS, PAGES_PER_STEP * PAGE_SIZE),
                                  lambda b, c, pt: (layer, b, 0, c)),
                     per_b((LANES, D_ATTN)), per_b((LANES, D_ATTN)),
                     per_b((rows, LANES)), per_b((rows, LANES)),
                     pl.BlockSpec((t_len, D_ATTN), lambda b, c, pt: (b, COL_GA // D_ATTN))]),
        out_specs=pl.BlockSpec((t_len, D_ATTN), lambda b, c, pt: (b, 0)),
        scratch_shapes=[pltpu.VMEM((rows, 1), F32), pltpu.VMEM((rows, 1), F32),
                        pltpu.VMEM((rows, D_ATTN), F32)],
    )
    ck = cache_k.reshape(cache_k.shape[0], cache_k.shape[1], PAGE_SIZE, D_ATTN)
    cv = cache_v.reshape(cache_v.shape[0], cache_v.shape[1], PAGE_SIZE, D_ATTN)
    return pl.pallas_call(
        _foxdec_kernel,
        grid_spec=grid_spec,
        out_shape=jax.ShapeDtypeStruct((n_dec * t_len, D_ATTN), F32),
        compiler_params=_cparams(("arbitrary", "arbitrary"), 56),
        name="foxdec",
    )(page_table, qbd, *([ck] * PAGES_PER_STEP), *([cv] * PAGES_PER_STEP),
      srev, knew, vnew, cn_col, cn_row, z)


def _s5prep_kernel(lr_ref, li_ref, ldt_ref, lrx_ref, lix_ref, bre_ref, bim_ref,
                   abr_ref, abi_ref, bbr_ref, bbi_ref):
    dt = jnp.exp(ldt_ref[...])

    def discretise(lr, li):
        mag = jnp.exp(lr * dt)
        abr = mag * jnp.cos(li * dt)
        abi = mag * jnp.sin(li * dt)
        den = lr * lr + li * li
        pr = abr - 1.0
        return abr, abi, (pr * lr + abi * li) / den, (abi * lr - pr * li) / den

    abr, abi, _, _ = discretise(lr_ref[...], li_ref[...])
    abr_ref[...] = abr
    abi_ref[...] = abi
    _, _, fr, fi = discretise(lrx_ref[...], lix_ref[...])
    bre, bim = bre_ref[...], bim_ref[...]
    bbr_ref[...] = fr * bre - fi * bim
    bbi_ref[...] = fr * bim + fi * bre


def _s5prep_all(lam_re, lam_im, log_dt, b_re, b_im):
    depth, g, n = lam_re.shape
    c = b_re.shape[-1]
    small = pl.BlockSpec((None, g, n), lambda l: (l, 0, 0))
    wide = pl.BlockSpec((None, g, n * c), lambda l: (l, 0, 0))
    return pl.pallas_call(
        _s5prep_kernel,
        grid=(depth,),
        in_specs=[small, small, pl.BlockSpec((None, g, 1), lambda l: (l, 0, 0)), wide, wide, wide, wide],
        out_specs=[small, small, wide, wide],
        out_shape=[jax.ShapeDtypeStruct((depth, g, n), F32)] * 2 + [jax.ShapeDtypeStruct((depth, g, n * c), F32)] * 2,
        compiler_params=_cparams(("arbitrary",), 32),
        name="s5prep",
    )(lam_re, lam_im, log_dt.reshape(depth, g, 1),
      jnp.repeat(lam_re, c, axis=-1), jnp.repeat(lam_im, c, axis=-1),
      b_re.reshape(depth, g, n * c), b_im.reshape(depth, g, n * c))


def _s5_kernel(x_ref, h0_ref, bset_ref, cset_ref, ar_ref, ai_ref, dsk_ref, wglu_ref, bglu_ref,
               y_ref, hfin_ref, u_ref, hst_ref):
    rows = x_ref.shape[0]
    nb = SUBLANES
    half = SET_H // 2

    @pl.when(pl.program_id(0) == 0)
    def _():
        hst_ref[...] = h0_ref[...]

    x = x_ref[...]
    xb = x.astype(BF16)
    for s in range(S5_SETS):
        u_ref[:, s * SET_H:(s + 1) * SET_H] = _dot(xb[:, s * SET_X:(s + 1) * SET_X], bset_ref[s])

    ar, ai = ar_ref[...], ai_ref[...]

    def step(t, h):
        r0 = pl.multiple_of(t * nb, nb)
        u = u_ref[pl.ds(r0, nb), :]
        out = []
        for s in range(S5_SETS):
            o = s * SET_H
            hr, hi = h[:, o:o + half], h[:, o + half:o + SET_H]
            a_r, a_i = ar[:, s * half:(s + 1) * half], ai[:, s * half:(s + 1) * half]
            out.append(a_r * hr - a_i * hi + u[:, o:o + half])
            out.append(a_r * hi + a_i * hr + u[:, o + half:o + SET_H])
        hn = jnp.concatenate(out, axis=1)
        u_ref[pl.ds(r0, nb), :] = hn
        return hn

    h = lax.fori_loop(0, rows // nb, step, hst_ref[...])
    hst_ref[...] = h
    hfin_ref[...] = h

    hb = u_ref[...].astype(BF16)
    y = jnp.concatenate([_dot(hb[:, s * SET_H:(s + 1) * SET_H], cset_ref[s]) for s in range(S5_SETS)], axis=1)
    y = y + dsk_ref[...] * x
    y = 0.5 * y * (1.0 + jnp.tanh(0.7978845608028654 * (y + 0.044715 * (y * y * y))))
    y_ref[...] = y * jax.nn.sigmoid(_dot(y.astype(BF16), wglu_ref[...]) + bglu_ref[...])


def _s5(x_tb, h0, bset, cset, ar, ai, d_skip, w_glu, b_glu, tt):
    rows = x_tb.shape[0]
    blk = tt * SUBLANES
    const2 = lambda shape: pl.BlockSpec(shape, lambda i: (0, 0))
    const3 = lambda shape: pl.BlockSpec(shape, lambda i: (0, 0, 0))
    return pl.pallas_call(
        _s5_kernel,
        grid=(rows // blk,),
        in_specs=[pl.BlockSpec((blk, D_SSM), lambda i: (i, 0)),
                  const2((SUBLANES, D_STATE)),
                  const3((S5_SETS, SET_X, SET_H)), const3((S5_SETS, SET_H, SET_X)),
                  const2((SUBLANES, D_STATE // 2)), const2((SUBLANES, D_STATE // 2)),
                  const2((1, D_SSM)), const2((D_SSM, D_SSM)), const2((1, D_SSM))],
        out_specs=[pl.BlockSpec((blk, D_SSM), lambda i: (i, 0)), const2((SUBLANES, D_STATE))],
        out_shape=[jax.ShapeDtypeStruct((rows, D_SSM), F32), jax.ShapeDtypeStruct((SUBLANES, D_STATE), F32)],
        scratch_shapes=[pltpu.VMEM((blk, D_STATE), F32), pltpu.VMEM((SUBLANES, D_STATE), F32)],
        compiler_params=_cparams(("arbitrary",), 48),
        name="s5",
    )(x_tb, h0, bset, cset, ar, ai, d_skip, w_glu, b_glu)


def _s5_operands(abr, abi, bbr, bbi, c_re, c_im):
    g, n, c = N_SSM_GROUPS, SSM_STATE, SSM_GROUP
    gs = g // S5_SETS
    eye = jnp.eye(gs, dtype=F32)

    def in_map(bb):
        bb = bb.reshape(S5_SETS, gs, n, c).transpose(0, 1, 3, 2)
        return (bb[:, :, :, None, :] * eye[None, :, None, :, None]).reshape(S5_SETS, gs * c, gs * n)

    def out_map(cc):
        cc = cc.reshape(S5_SETS, gs, c, n).transpose(0, 1, 3, 2)
        return (cc[:, :, :, None, :] * eye[None, :, None, :, None]).reshape(S5_SETS, gs * n, gs * c)

    bset = jnp.concatenate([in_map(bbr), in_map(bbi)], axis=2).astype(BF16)
    cset = jnp.concatenate([out_map(c_re), -out_map(c_im)], axis=1).astype(BF16)
    ar = jnp.broadcast_to(abr.reshape(1, g * n), (SUBLANES, g * n))
    ai = jnp.broadcast_to(abi.reshape(1, g * n), (SUBLANES, g * n))
    return bset, cset, ar, ai


def _state_to_lanes(h_re, h_im):
    nb = h_re.shape[0]
    gs = N_SSM_GROUPS // S5_SETS
    st = jnp.stack([h_re.reshape(nb, S5_SETS, gs, SSM_STATE), h_im.reshape(nb, S5_SETS, gs, SSM_STATE)], axis=2)
    return st.reshape(nb, D_STATE)


def _lanes_to_state(h):
    nb = h.shape[0]
    gs = N_SSM_GROUPS // S5_SETS
    st = h.reshape(nb, S5_SETS, 2, gs, SSM_STATE)
    return (st[:, :, 0].reshape(nb, N_SSM_GROUPS, SSM_STATE), st[:, :, 1].reshape(nb, N_SSM_GROUPS, SSM_STATE))


def _cmlp_kernel(lc, emit_vn, u_ref, v_ref, gm_ref, g_ref, b_ref, w_ref, bias_ref, o_ref, *vn_ref):
    tm = u_ref.shape[0]
    v = v_ref[...]
    mu = jnp.mean(v, axis=-1, keepdims=True)
    d = v - mu
    var = jnp.mean(d * d, axis=-1, keepdims=True)
    vn = d * lax.rsqrt(var + EPS) * g_ref[...] + b_ref[...]
    if emit_vn:
        vn_ref[0][...] = vn
    row = lax.broadcasted_iota(jnp.int32, (lc, lc), 0)
    col = lax.broadcasted_iota(jnp.int32, (lc, lc), 1)
    ug = u_ref[...] * _silu(gm_ref[...])
    vb = vn.astype(BF16)
    for g in range(N_CMLP_GROUPS):
        wg = jnp.where(col <= row, w_ref[g], 0.0).astype(BF16)
        lanes = slice(g * CMLP_GROUP, (g + 1) * CMLP_GROUP)
        for c in range(tm // lc):
            rws = slice(c * lc, (c + 1) * lc)
            zc = _dot(wg, vb[rws, lanes]) + bias_ref[:, lanes]
            o_ref[rws, lanes] = (ug[rws, lanes] * zc).astype(BF16)


def _cmlp(z, ln_g, ln_b, w, bias, tm, lc, emit_vn):
    t = z.shape[0]
    col = lambda c: pl.BlockSpec((tm, D_CMLP), lambda i: (i, c))
    const2 = lambda shape: pl.BlockSpec(shape, lambda i: (0, 0))
    out_specs = [pl.BlockSpec((tm, D_CMLP), lambda i: (i, 0))]
    out_shape = [jax.ShapeDtypeStruct((t, D_CMLP), BF16)]
    if emit_vn:
        out_specs.append(pl.BlockSpec((tm, D_CMLP), lambda i: (i, 0)))
        out_shape.append(jax.ShapeDtypeStruct((t, D_CMLP), F32))
    return pl.pallas_call(
        functools.partial(_cmlp_kernel, lc, emit_vn),
        grid=(t // tm,),
        in_specs=[col(COL_U // D_CMLP), col(COL_VV // D_CMLP), col(COL_GM // D_CMLP),
                  const2((1, D_CMLP)), const2((1, D_CMLP)),
                  pl.BlockSpec((N_CMLP_GROUPS, lc, lc), lambda i: (0, 0, 0)),
                  const2((lc, D_CMLP))],
        out_specs=out_specs,
        out_shape=out_shape,
        compiler_params=_cparams(("arbitrary",), 32),
        name="cmlp",
    )(z, z, z, ln_g, ln_b, w, bias)


def _outproj_kernel(x_ref, a_ref, s_ref, gs_ref, m_ref, gate_ref, w_ref, o_ref):
    sg = (s_ref[...] * _silu(gs_ref[...])).astype(BF16)
    acc = (_dot(a_ref[...].astype(BF16), w_ref[0:D_ATTN, :])
           + _dot(sg, w_ref[D_ATTN:D_ATTN + D_SSM, :])
           + _dot(m_ref[...], w_ref[D_ATTN + D_SSM:, :]))
    o_ref[...] = x_ref[...] + gate_ref[...] * acc


def _outproj(x, a, s, z, m, gate, w_out, tm, rows_per_mod):
    t = x.shape[0]
    mod_rows = gate.shape[1]
    return pl.pallas_call(
        _outproj_kernel,
        grid=(t // tm,),
        in_specs=[pl.BlockSpec((tm, D_MODEL), lambda i: (i, 0)),
                  pl.BlockSpec((tm, D_ATTN), lambda i: (i, 0)),
                  pl.BlockSpec((tm, D_SSM), lambda i: (i, 0)),
                  pl.BlockSpec((tm, D_SSM), lambda i: (i, COL_GS // D_SSM)),
                  pl.BlockSpec((tm, D_CMLP), lambda i: (i, 0)),
                  pl.BlockSpec((None, mod_rows, D_MODEL), lambda i: ((i * tm) // rows_per_mod, 0, 0)),
                  pl.BlockSpec((D_MODEL, D_MODEL), lambda i: (0, 0))],
        out_specs=pl.BlockSpec((tm, D_MODEL), lambda i: (i, 0)),
        out_shape=jax.ShapeDtypeStruct((t, D_MODEL), F32),
        compiler_params=_cparams(("arbitrary",), 56),
        name="outproj",
    )(x, a, s, z, m, gate, w_out)


def _pack_w_in(w_in):
    d_q = 3 * D_ATTN
    fg = w_in[:, :, d_q:d_q + N_HEADS]
    main = jnp.concatenate([w_in[:, :, :d_q], w_in[:, :, d_q + N_HEADS:]], axis=-1).astype(BF16)
    w_fg = jnp.pad(fg, ((0, 0), (0, 0), (0, LANES - N_HEADS))).astype(BF16)
    return main, w_fg


def _layer(x, mods, attn_fn, h0_lanes, seq_len, batch, wl, consts, tm, tt, lc, emit_vn):
    t = x.shape[0]
    shift, scale, gate, rows_per_mod = mods
    z, logf = _inproj(x, wl["norm_g"], scale, shift, wl["w_main"], wl["w_fg"], wl["b_f"], tm, rows_per_mod)
    qa, ka, kn, vb, f = _qkprep(z, logf, wl["qg"], wl["kg"], consts, min(tm, 256), seq_len)
    a_out = attn_fn(qa, ka, vb, f, z)

    xs = z[:, COL_XS:COL_XS + D_SSM].reshape(batch, seq_len, D_SSM).transpose(1, 0, 2)
    xs = jnp.pad(xs, ((0, 0), (0, SUBLANES - batch), (0, 0))).reshape(seq_len * SUBLANES, D_SSM)
    s_tb, hfin = _s5(xs, h0_lanes, wl["bset"], wl["cset"], wl["ar"], wl["ai"], wl["d_skip"],
                     wl["w_glu"], wl["b_glu"], tt)
    s_pre = s_tb.reshape(seq_len, SUBLANES, D_SSM)[:, :batch].transpose(1, 0, 2).reshape(t, D_SSM)

    m_res = _cmlp(z, wl["ln_g"], wl["ln_b"], wl["w_s"], wl["b_s"], min(tm, 512), lc, emit_vn)
    y = _outproj(x, a_out, s_pre, z, m_res[0], gate, wl["w_out"], min(tm, 512), rows_per_mod)
    h_re, h_im = _lanes_to_state(hfin[:batch])
    v = z[:, COL_V:COL_V + D_ATTN]
    return y, (kn, v, logf[:, :N_HEADS], h_re, h_im, m_res[1] if emit_vn else None)


def kernel(x_prompt, x_sample, c_prompt, c_sample, cache_k, cache_v, cache_logf, state_ssm_re, state_ssm_im,
           page_table, norm_g, w_ada, b_ada, w_in, b_f, q_norm_g, k_norm_g, lam_re, lam_im, log_dt,
           b_re, b_im, c_re, c_im, d_skip, w_glu, b_glu, sgu_ln_g, sgu_ln_b, w_s, b_s, w_out):
    depth = w_in.shape[0]
    bp, seq, _ = x_prompt.shape
    bd, t_dec, _ = x_sample.shape
    n_dec, n_pages = page_table.shape

    c_all = jnp.concatenate([c_prompt, c_sample], axis=0)
    c_all = jnp.pad(c_all, ((0, 2 * SUBLANES - bp - bd), (0, 0)))
    mod = _ada_all(c_all, w_ada, b_ada)

    w_main, w_fg = _pack_w_in(w_in)
    w_out_b = w_out.astype(BF16)
    w_glu_b = w_glu.astype(BF16)
    b_f_pad = jnp.pad(b_f, ((0, 0), (0, LANES - N_HEADS))).reshape(depth, 1, LANES)
    consts = _aug_placement()
    abr, abi, bbr, bbi = _s5prep_all(lam_re, lam_im, log_dt, b_re, b_im)
    srev = _srev_all(cache_logf, page_table)

    eye_d = jnp.eye(bd, dtype=F32)
    w_s_dec = (eye_d[None, None, :, None, :, None] * w_s[:, :, None, :t_dec, None, :t_dec]).reshape(
        depth, N_CMLP_GROUPS, bd * t_dec, bd * t_dec)
    bias_p = jnp.repeat(b_s.transpose(0, 2, 1), CMLP_GROUP, axis=-1)
    bias_d = jnp.tile(bias_p[:, :t_dec], (1, bd, 1))

    yp = x_prompt.reshape(bp * seq, D_MODEL)
    ys = x_sample.reshape(bd * t_dec, D_MODEL)
    h_zero = jnp.zeros((SUBLANES, D_STATE), F32)
    eye_h = jnp.eye(N_HEADS, dtype=BF16)
    outs_p, outs_s = [], []
    for l in range(depth):
        bset, cset, ar, ai = _s5_operands(abr[l], abi[l], bbr[l], bbi[l], c_re[l], c_im[l])
        wl = dict(norm_g=norm_g[l][None], w_main=w_main[l], w_fg=w_fg[l], b_f=b_f_pad[l],
                  qg=jnp.tile(q_norm_g[l], N_HEADS)[None], kg=jnp.tile(k_norm_g[l], N_HEADS)[None],
                  bset=bset, cset=cset, ar=ar, ai=ai, d_skip=d_skip[l][None], w_glu=w_glu_b[l],
                  b_glu=b_glu[l][None], ln_g=sgu_ln_g[l][None], ln_b=sgu_ln_b[l][None], w_out=w_out_b[l])
        shift, scale, gate = (mod[l][:, i * D_MODEL:(i + 1) * D_MODEL] for i in range(3))

        mods_p = (shift[:bp, None], scale[:bp, None], gate[:bp, None], seq)
        attn_p = lambda qa, ka, vb, f, z: _fox_prompt(qa, ka, vb, z, bp, seq, 512)
        yp, cache_p = _layer(yp, mods_p, attn_p, h_zero, seq, bp, dict(wl, w_s=w_s[l], b_s=bias_p[l]),
                             consts, 1024, 64, CHUNK, False)
        outs_p.append(cache_p)

        rep = lambda m: jnp.repeat(m[bp:bp + bd], t_dec, axis=0)[None]
        mods_s = (rep(shift), rep(scale), rep(gate), bd * t_dec)

        def attn_s(qa, ka, vb, f, z, l=l):
            q = qa.reshape(bd, t_dec, N_HEADS, 2 * HEAD_DIM)[..., :HEAD_DIM]
            qbd = (q[:, :, :, None, :] * eye_h[None, None, :, :, None]).reshape(bd, t_dec * N_HEADS, D_ATTN)
            kb = ka.reshape(bd, t_dec, N_HEADS, 2 * HEAD_DIM)[..., :HEAD_DIM].reshape(bd, t_dec, D_ATTN)
            knew = jnp.pad(kb, ((0, 0), (0, LANES - t_dec), (0, 0)))
            vnew = jnp.pad(vb.reshape(bd, t_dec, D_ATTN), ((0, 0), (0, LANES - t_dec), (0, 0)))
            cn = f[:, :N_HEADS].reshape(bd, t_dec, N_HEADS)
            cn_col = jnp.broadcast_to(cn.reshape(bd, t_dec * N_HEADS, 1), (bd, t_dec * N_HEADS, LANES))
            cn_row = jnp.broadcast_to(cn.transpose(0, 2, 1)[:, None], (bd, t_dec, N_HEADS, t_dec))
            cn_row = jnp.pad(cn_row.reshape(bd, t_dec * N_HEADS, t_dec), ((0, 0), (0, 0), (0, LANES - t_dec)))
            return _fox_sample(l, page_table, qbd, cache_k, cache_v, srev, knew, vnew, cn_col, cn_row, z)

        h0 = _state_to_lanes(state_ssm_re[:, l], state_ssm_im[:, l])
        ys, cache_s = _layer(ys, mods_s, attn_s, h0, t_dec, bd, dict(wl, w_s=w_s_dec[l], b_s=bias_d[l]),
                             consts, bd * t_dec, t_dec, bd * t_dec, True)
        outs_s.append(cache_s)

    def stack(outs, idx, shape):
        return jnp.stack([o[idx].reshape(shape) for o in outs], axis=1)

    return (yp.reshape(bp, seq, D_MODEL), ys.reshape(bd, t_dec, D_MODEL),
            stack(outs_p, 0, (bp, seq, N_HEADS, HEAD_DIM)), stack(outs_p, 1, (bp, seq, N_HEADS, HEAD_DIM)),
            stack(outs_p, 2, (bp, seq, N_HEADS)),
            stack(outs_p, 3, (bp, N_SSM_GROUPS, SSM_STATE)), stack(outs_p, 4, (bp, N_SSM_GROUPS, SSM_STATE)),
            stack(outs_s, 0, (bd, t_dec, N_HEADS, HEAD_DIM)), stack(outs_s, 1, (bd, t_dec, N_HEADS, HEAD_DIM)),
            stack(outs_s, 2, (bd, t_dec, N_HEADS)),
            stack(outs_s, 3, (bd, N_SSM_GROUPS, SSM_STATE)), stack(outs_s, 4, (bd, N_SSM_GROUPS, SSM_STATE)),
            stack(outs_s, 5, (bd, t_dec, D_CMLP)))
```

```python
import functools

import jax
import jax.numpy as jnp
from jax import lax
from jax.experimental import pallas as pl
from jax.experimental.pallas import tpu as pltpu

F32 = jnp.float32
BF16 = jnp.bfloat16

D_MODEL = 2048
HEAD_DIM = 64
D_ATTN = 1024
N_HEADS = 16
D_SSM = 512
SSM_GROUP = 16
N_SSM_GROUPS = 32
SSM_STATE = 64
D_CMLP = 512
CHUNK = 128
N_CMLP_GROUPS = 4
CMLP_GROUP = 128
PAGE_SIZE = 128
EPS = 1e-6
LANES = 128
SUBLANES = 8
NEG_BIG = -1e30

COL_Q, COL_K, COL_V, COL_GA = 0, 1024, 2048, 3072
COL_XS, COL_GS, COL_U, COL_VV, COL_GM = 4096, 4608, 5120, 5632, 6144
D_Z = 6656
D_STATE = 2 * N_SSM_GROUPS * SSM_STATE
S5_SETS = 2
SET_X = D_SSM // S5_SETS
SET_H = D_STATE // S5_SETS
PAGES_PER_STEP = 4
SREV_PAGES_PER_STEP = 16
LOG2E = 1.4426950408889634
TQ = 512


def _cparams(sem, vmem_mb):
    return pltpu.CompilerParams(dimension_semantics=sem, vmem_limit_bytes=vmem_mb * 1024 * 1024)


def _split3(x):
    hi = x.astype(BF16)
    r1 = x - hi.astype(F32)
    mid = r1.astype(BF16)
    lo = (r1 - mid.astype(F32)).astype(BF16)
    return hi, mid, lo


def _dot(a, b):
    return jnp.dot(a, b, preferred_element_type=F32)


def _dot_nt(a, b):
    return lax.dot_general(a, b, (((1,), (1,)), ((), ())), preferred_element_type=F32)


def _silu(x):
    return x * jax.nn.sigmoid(x)


def _ada_kernel(c_ref, w_ref, b_ref, o_ref):
    s = _silu(c_ref[...]).astype(BF16)
    o_ref[...] = _dot(s, w_ref[...].astype(BF16)) + b_ref[...]


def _ada_all(c_all, w_ada, b_ada):
    depth = w_ada.shape[0]
    rows = c_all.shape[0]
    tn = 1024
    return pl.pallas_call(
        _ada_kernel,
        grid=(depth, 3 * D_MODEL // tn),
        in_specs=[
            pl.BlockSpec((rows, D_MODEL), lambda l, j: (0, 0)),
            pl.BlockSpec((None, D_MODEL, tn), lambda l, j: (l, 0, j)),
            pl.BlockSpec((None, 1, tn), lambda l, j: (l, 0, j)),
        ],
        out_specs=pl.BlockSpec((None, rows, tn), lambda l, j: (l, 0, j)),
        out_shape=jax.ShapeDtypeStruct((depth, rows, 3 * D_MODEL), F32),
        compiler_params=_cparams(("arbitrary", "arbitrary"), 48),
        name="ada",
    )(c_all, w_ada, b_ada.reshape(depth, 1, 3 * D_MODEL))


def _inproj_kernel(x_ref, g_ref, sc_ref, sh_ref, w_ref, wfg_ref, bf_ref, z_ref, lf_ref, h_ref):
    @pl.when(pl.program_id(1) == 0)
    def _():
        x = x_ref[...]
        ms = jnp.mean(x * x, axis=-1, keepdims=True)
        y = x * lax.rsqrt(ms + EPS) * g_ref[...]
        hb = (y * (1.0 + sc_ref[...]) + sh_ref[...]).astype(BF16)
        h_ref[...] = hb
        fg = _dot(hb, wfg_ref[...]) + bf_ref[...]
        lf_ref[...] = jnp.minimum(fg, 0.0) - jnp.log1p(jnp.exp(-jnp.abs(fg)))

    z_ref[...] = _dot(h_ref[...], w_ref[...])


def _inproj(x, norm_g, scale, shift, w_main, w_fg, b_f, tm, rows_per_mod):
    t = x.shape[0]
    tn = 512
    mod_rows = scale.shape[1]
    mod_spec = pl.BlockSpec((None, mod_rows, D_MODEL), lambda i, j: ((i * tm) // rows_per_mod, 0, 0))
    return pl.pallas_call(
        _inproj_kernel,
        grid=(t // tm, D_Z // tn),
        in_specs=[
            pl.BlockSpec((tm, D_MODEL), lambda i, j: (i, 0)),
            pl.BlockSpec((1, D_MODEL), lambda i, j: (0, 0)),
            mod_spec,
            mod_spec,
            pl.BlockSpec((D_MODEL, tn), lambda i, j: (0, j)),
            pl.BlockSpec((D_MODEL, LANES), lambda i, j: (0, 0)),
            pl.BlockSpec((1, LANES), lambda i, j: (0, 0)),
        ],
        out_specs=[
            pl.BlockSpec((tm, tn), lambda i, j: (i, j)),
            pl.BlockSpec((tm, LANES), lambda i, j: (i, 0)),
        ],
        out_shape=[jax.ShapeDtypeStruct((t, D_Z), F32), jax.ShapeDtypeStruct((t, LANES), F32)],
        scratch_shapes=[pltpu.VMEM((tm, D_MODEL), BF16)],
        compiler_params=_cparams(("arbitrary", "arbitrary"), 48),
        name="inproj",
    )(x, norm_g, scale, shift, w_main, w_fg, b_f)


def _qkprep_kernel(seq_len, emit_vt, q_ref, k_ref, v_ref, lf_ref, qg_ref, kg_ref, seg_ref, pq_ref, pk_ref,
                   qa_ref, ka_ref, kn_ref, v_out_ref, f_ref, carry_ref):
    tm = q_ref.shape[0]
    row = lax.broadcasted_iota(jnp.int32, (tm, tm), 0)
    col = lax.broadcasted_iota(jnp.int32, (tm, tm), 1)
    tri = col <= row
    if seq_len < tm:
        shift = seq_len.bit_length() - 1
        tri = tri & (lax.shift_right_logical(row, shift) == lax.shift_right_logical(col, shift))
    tri_b = jnp.where(tri, 1.0, 0.0).astype(BF16)
    hi, mid, lo = _split3(lf_ref[...])
    f = _dot(tri_b, hi) + _dot(tri_b, mid) + _dot(tri_b, lo)
    if seq_len > tm:
        @pl.when(pl.program_id(0) % (seq_len // tm) == 0)
        def _():
            carry_ref[...] = jnp.zeros_like(carry_ref)

        f = f + carry_ref[...]
        carry_ref[...] = f[tm - 1:tm, :]
    f_ref[...] = f

    def head_norm(x, g):
        ss = _dot((x * x).astype(BF16), seg_ref[...])
        return x * lax.rsqrt(ss * (1.0 / HEAD_DIM) + EPS) * g

    qs = head_norm(q_ref[...], qg_ref[...]) * (HEAD_DIM ** -0.5 * LOG2E)
    kn = head_norm(k_ref[...], kg_ref[...])
    kn_ref[...] = kn
    v = v_ref[...]
    if emit_vt:
        for p in range(N_HEADS // 2):
            v_out_ref[p] = v[:, p * LANES:(p + 1) * LANES].T.astype(BF16)
    else:
        v_out_ref[...] = v.astype(BF16)

    fh, fm, fl = _split3(f * LOG2E)
    lane = lax.broadcasted_iota(jnp.int32, (tm, LANES), 1)
    parts = jnp.where(lane < 16, fh.astype(F32),
                      jnp.where(lane < 32, pltpu.roll(fm.astype(F32), 16, 1),
                                jnp.where(lane < 48, pltpu.roll(fl.astype(F32), 32, 1),
                                          jnp.where(lane == 48, 1.0, 0.0)))).astype(BF16)
    ext_q = _dot(parts, pq_ref[...])
    ext_k = _dot(parts, pk_ref[...])
    low = lane < HEAD_DIM
    for src, ext, dst in ((qs, ext_q, qa_ref), (kn, ext_k, ka_ref)):
        for p in range(N_HEADS // 2):
            blk = src[:, p * LANES:(p + 1) * LANES]
            e = ext[:, p * LANES:(p + 1) * LANES]
            dst[:, (2 * p) * LANES:(2 * p + 1) * LANES] = jnp.where(low, blk, pltpu.roll(e, 64, 1)).astype(BF16)
            dst[:, (2 * p + 1) * LANES:(2 * p + 2) * LANES] = jnp.where(low, pltpu.roll(blk, 64, 1), e).astype(BF16)


def _aug_placement():
    r = jnp.arange(LANES)[:, None]
    c = jnp.arange(N_HEADS * HEAD_DIM)[None, :]
    rp, rh = r // 16, r % 16
    ch, cc = c // HEAD_DIM, c % HEAD_DIM
    f_rows = (r < 48) & (rh == ch)
    one_row = r == 48
    pq = jnp.where(f_rows & (cc == rp), 1.0, 0.0) + jnp.where(one_row & (cc >= 3) & (cc < 6), 1.0, 0.0)
    pk = jnp.where(f_rows & (cc == rp + 3), -1.0, 0.0) + jnp.where(one_row & (cc < 3), 1.0, 0.0)
    seg = jnp.where(jnp.arange(D_ATTN)[:, None] // HEAD_DIM == jnp.arange(D_ATTN)[None, :] // HEAD_DIM, 1.0, 0.0)
    return pq.astype(BF16), pk.astype(BF16), seg.astype(BF16)


def _qkprep(z, logf, qg, kg, consts, tm, seq_len, emit_vt):
    t = z.shape[0]
    pq, pk, seg = consts
    col = lambda c: pl.BlockSpec((tm, D_ATTN), lambda i: (i, c))
    const = lambda shape: pl.BlockSpec(shape, lambda i: (0, 0))
    pairs = N_HEADS // 2
    if emit_vt:
        nk = seq_len // tm
        v_spec = pl.BlockSpec((None, pairs, None, LANES, tm), lambda i: (i // nk, 0, i % nk, 0, 0))
        v_shape = jax.ShapeDtypeStruct((t // seq_len, pairs, nk, LANES, tm), BF16)
    else:
        v_spec = pl.BlockSpec((tm, D_ATTN), lambda i: (i, 0))
        v_shape = jax.ShapeDtypeStruct((t, D_ATTN), BF16)
    return pl.pallas_call(
        functools.partial(_qkprep_kernel, seq_len, emit_vt),
        grid=(t // tm,),
        in_specs=[col(COL_Q // D_ATTN), col(COL_K // D_ATTN), col(COL_V // D_ATTN),
                  pl.BlockSpec((tm, LANES), lambda i: (i, 0)),
                  const((1, D_ATTN)), const((1, D_ATTN)), const((D_ATTN, D_ATTN)),
                  const((LANES, D_ATTN)), const((LANES, D_ATTN))],
        out_specs=[pl.BlockSpec((tm, 2 * D_ATTN), lambda i: (i, 0)),
                   pl.BlockSpec((tm, 2 * D_ATTN), lambda i: (i, 0)),
                   pl.BlockSpec((tm, D_ATTN), lambda i: (i, 0)),
                   v_spec,
                   pl.BlockSpec((tm, LANES), lambda i: (i, 0))],
        out_shape=[jax.ShapeDtypeStruct((t, 2 * D_ATTN), BF16), jax.ShapeDtypeStruct((t, 2 * D_ATTN), BF16),
                   jax.ShapeDtypeStruct((t, D_ATTN), F32), v_shape,
                   jax.ShapeDtypeStruct((t, LANES), F32)],
        scratch_shapes=[pltpu.VMEM((1, LANES), F32)],
        compiler_params=_cparams(("arbitrary",), 56),
        name="qkprep",
    )(z, z, z, logf, qg, kg, seg, pq, pk)


def _softmax_absorb(s, vt, m_ref, l_ref, acc_ref, idx):
    m_prev = m_ref[idx]
    m_new = jnp.maximum(m_prev, jnp.max(s, axis=0, keepdims=True))
    alpha = jnp.exp2(m_prev - m_new)
    p = jnp.exp2(s - m_new)
    l_ref[idx] = alpha * l_ref[idx] + jnp.sum(p, axis=0, keepdims=True)
    acc_ref[idx] = alpha * acc_ref[idx] + _dot(vt, p.astype(BF16))
    m_ref[idx] = m_new


def _fox_kernel(qa_ref, ka_ref, vt_ref, ga_ref, o_ref, m_ref, l_ref, acc_ref):
    i = pl.program_id(2)
    tq = qa_ref.shape[0]
    tk = vt_ref.shape[2]
    m_ref[...] = jnp.full_like(m_ref, NEG_BIG)
    l_ref[...] = jnp.zeros_like(l_ref)
    acc_ref[...] = jnp.zeros_like(acc_ref)

    def tile(j, masked):
        k0 = pl.multiple_of(j * tk, tk)
        for hh in range(2):
            k = ka_ref[pl.ds(k0, tk), hh * LANES:(hh + 1) * LANES]
            s = _dot_nt(k, qa_ref[:, hh * LANES:(hh + 1) * LANES])
            if masked:
                row = lax.broadcasted_iota(jnp.int32, (tk, tq), 0)
                col = lax.broadcasted_iota(jnp.int32, (tk, tq), 1)
                s = jnp.where(row <= col, s, NEG_BIG)
            _softmax_absorb(s, vt_ref[j, hh * HEAD_DIM:(hh + 1) * HEAD_DIM, :], m_ref, l_ref, acc_ref, hh)

    def body(j, carry):
        tile(j, False)
        return carry

    lax.fori_loop(0, i, body, 0)
    tile(i, True)
    o_t = jnp.concatenate([acc_ref[0] / l_ref[0], acc_ref[1] / l_ref[1]], axis=0)
    o_ref[...] = (o_t.T * _silu(ga_ref[...])).astype(BF16)


def _fox_prompt(qa, ka, vt, z, batch, seq, tq):
    nq = seq // tq
    pairs = N_HEADS // 2
    ga_blk = COL_GA // LANES
    return pl.pallas_call(
        _fox_kernel,
        grid=(batch, pairs, nq),
        in_specs=[
            pl.BlockSpec((tq, 2 * LANES), lambda b, p, i: (b * nq + i, p)),
            pl.BlockSpec((seq, 2 * LANES), lambda b, p, i: (b, p)),
            pl.BlockSpec((None, None, nq, LANES, tq), lambda b, p, i: (b, p, 0, 0, 0)),
            pl.BlockSpec((tq, LANES), lambda b, p, i: (b * nq + i, ga_blk + p)),
        ],
        out_specs=pl.BlockSpec((tq, LANES), lambda b, p, i: (b * nq + i, p)),
        out_shape=jax.ShapeDtypeStruct((batch * seq, D_ATTN), BF16),
        scratch_shapes=[pltpu.VMEM((2, 1, tq), F32), pltpu.VMEM((2, 1, tq), F32),
                        pltpu.VMEM((2, HEAD_DIM, tq), F32)],
        compiler_params=_cparams(("arbitrary",) * 3, 48),
        name="fox",
    )(qa, ka, vt, z)


def _srev_kernel(pt_ref, *refs):
    n = SREV_PAGES_PER_STEP
    lf_refs = refs[:n]
    e8_ref, out_ref, pad_ref, carry_ref = refs[n:]

    @pl.when(pl.program_id(2) == 0)
    def _():
        carry_ref[...] = jnp.zeros_like(carry_ref)
        pad_ref[...] = jnp.zeros_like(pad_ref)

    row = lax.broadcasted_iota(jnp.int32, (PAGE_SIZE, PAGE_SIZE), 0)
    col = lax.broadcasted_iota(jnp.int32, (PAGE_SIZE, PAGE_SIZE), 1)
    later = jnp.where(col > row, 1.0, 0.0).astype(BF16)
    e8 = e8_ref[...]
    for i in reversed(range(n)):
        pad_ref[:, 0:N_HEADS] = lf_refs[i][...]
        x = pad_ref[...]
        hi, mid, lo = _split3(x)
        srev = _dot(later, hi) + _dot(later, mid) + _dot(later, lo) + carry_ref[...]
        carry_ref[...] = carry_ref[...] + jnp.sum(x, axis=0, keepdims=True)
        hi, mid, lo = _split3(srev)
        out_ref[i * PAGE_SIZE:(i + 1) * PAGE_SIZE, :] = _dot(hi, e8) + _dot(mid, e8) + _dot(lo, e8)


def _srev_all(cache_logf, page_table, t_dec):
    depth = cache_logf.shape[1]
    n_dec, n_pages = page_table.shape
    n = SREV_PAGES_PER_STEP
    nchunk = n_pages // n
    e8 = jnp.where(jnp.arange(LANES)[:, None] == jnp.arange(LANES)[None, :] // t_dec, 1.0, 0.0).astype(BF16)

    def page_spec(i):
        return pl.BlockSpec(
            (None, None, PAGE_SIZE, N_HEADS),
            lambda l, b, c, pt: (pt[b, (nchunk - 1 - c) * n + i], l, 0, 0))

    grid_spec = pltpu.PrefetchScalarGridSpec(
        num_scalar_prefetch=1,
        grid=(depth, n_dec, nchunk),
        in_specs=[page_spec(i) for i in range(n)] + [pl.BlockSpec((LANES, LANES), lambda l, b, c, pt: (0, 0))],
        out_specs=pl.BlockSpec((None, None, n * PAGE_SIZE, LANES), lambda l, b, c, pt: (l, b, nchunk - 1 - c, 0)),
        scratch_shapes=[pltpu.VMEM((PAGE_SIZE, LANES), F32), pltpu.VMEM((1, LANES), F32)],
    )
    return pl.pallas_call(
        _srev_kernel,
        grid_spec=grid_spec,
        out_shape=jax.ShapeDtypeStruct((depth, n_dec, n_pages * PAGE_SIZE, LANES), F32),
        compiler_params=_cparams(("arbitrary",) * 3, 32),
        name="srev",
    )(page_table, *([cache_logf] * n), e8)


def _foxdec_kernel(pt_ref, *refs):
    n = PAGES_PER_STEP
    w_ref = refs[0]
    k_refs = refs[1:1 + n]
    v_refs = refs[1 + n:1 + 2 * n]
    srev_ref, cn_ref, knew_ref, vnew_ref, bnew_ref, ga_ref, o_ref, m_ref, l_ref, acc_ref = refs[1 + 2 * n:]
    c = pl.program_id(1)
    t_dec = o_ref.shape[0]
    tshift = t_dec.bit_length() - 1
    rows = PAGE_SIZE * N_HEADS

    @pl.when(c == 0)
    def _():
        m_ref[...] = jnp.full_like(m_ref, NEG_BIG)
        l_ref[...] = jnp.zeros_like(l_ref)
        acc_ref[...] = jnp.zeros_like(acc_ref)

    def absorb(s, v_rows):
        m_prev = m_ref[...]
        m_new = jnp.maximum(m_prev, jnp.max(s, axis=0, keepdims=True))
        alpha = jnp.exp2(m_prev - m_new)
        p = jnp.exp2(s - m_new)
        l_ref[...] = alpha * l_ref[...] + jnp.sum(p, axis=0, keepdims=True)
        pv = lax.dot_general(v_rows, p.astype(BF16), (((0,), (0,)), ((), ())), preferred_element_type=F32)
        acc_ref[...] = alpha * acc_ref[...] + pv
        m_ref[...] = m_new

    w = w_ref[...]
    row = lax.broadcasted_iota(jnp.int32, (rows, LANES), 0)
    lane = lax.broadcasted_iota(jnp.int32, (rows, LANES), 1)
    own = (row & (N_HEADS - 1)) == lax.shift_right_logical(lane, tshift)
    cn_l = cn_ref[...] * LOG2E
    for pg in range(n):
        kb = k_refs[pg][...].reshape(rows, HEAD_DIM).astype(BF16)
        vb = v_refs[pg][...].reshape(rows, HEAD_DIM).astype(BF16)
        e = srev_ref[pg * PAGE_SIZE:(pg + 1) * PAGE_SIZE, :] * LOG2E + cn_l
        bias = jnp.broadcast_to(e[:, None, :], (PAGE_SIZE, N_HEADS, LANES)).reshape(rows, LANES)
        absorb(jnp.where(own, _dot(kb, w) + bias, NEG_BIG), vb)

    @pl.when(c == pl.num_programs(1) - 1)
    def _():
        nrows = knew_ref.shape[0]
        rown = lax.broadcasted_iota(jnp.int32, (nrows, LANES), 0)
        lanen = lax.broadcasted_iota(jnp.int32, (nrows, LANES), 1)
        valid = ((rown & (N_HEADS - 1)) == lax.shift_right_logical(lanen, tshift)) & (
            lax.shift_right_logical(rown, 4) <= (lanen & (t_dec - 1)))
        sn = _dot(knew_ref[...], w) + bnew_ref[...] * LOG2E
        absorb(jnp.where(valid, sn, NEG_BIG), vnew_ref[...])
        o_t = acc_ref[...] / l_ref[...]
        o_sq = jnp.concatenate([o_t, jnp.zeros_like(o_t)], axis=0).T
        o = jnp.concatenate([o_sq[h * t_dec:(h + 1) * t_dec, 0:HEAD_DIM] for h in range(N_HEADS)], axis=1)
        o_ref[...] = o * _silu(ga_ref[...])


def _fox_sample(layer, page_table, w, cache_k, cache_v, srev, cn_lane, knew, vnew, bias_new, z):
    n_dec, n_pages = page_table.shape
    n = PAGES_PER_STEP
    nchunk = n_pages // n
    t_dec = LANES // N_HEADS

    def page_spec(i):
        return pl.BlockSpec((None, None, PAGE_SIZE, N_HEADS, HEAD_DIM),
                            lambda b, c, pt: (pt[b, c * n + i], layer, 0, 0, 0))

    per_b = lambda shape: pl.BlockSpec((None,) + shape, lambda b, c, pt: (b, 0, 0))
    grid_spec = pltpu.PrefetchScalarGridSpec(
        num_scalar_prefetch=1,
        grid=(n_dec, nchunk),
        in_specs=([per_b((HEAD_DIM, LANES))]
                  + [page_spec(i) for i in range(n)]
                  + [page_spec(i) for i in range(n)]
                  + [pl.BlockSpec((None, None, n * PAGE_SIZE, LANES), lambda b, c, pt: (layer, b, c, 0)),
                     per_b((1, LANES)), per_b((LANES, HEAD_DIM)), per_b((LANES, HEAD_DIM)),
                     per_b((LANES, LANES)),
                     pl.BlockSpec((t_dec, D_ATTN), lambda b, c, pt: (b, COL_GA // D_ATTN))]),
        out_specs=pl.BlockSpec((t_dec, D_ATTN), lambda b, c, pt: (b, 0)),
        scratch_shapes=[pltpu.VMEM((1, LANES), F32), pltpu.VMEM((1, LANES), F32),
                        pltpu.VMEM((HEAD_DIM, LANES), F32)],
    )
    return pl.pallas_call(
        _foxdec_kernel,
        grid_spec=grid_spec,
        out_shape=jax.ShapeDtypeStruct((n_dec * t_dec, D_ATTN), F32),
        compiler_params=_cparams(("arbitrary", "arbitrary"), 56),
        name="foxdec",
    )(page_table, w, *([cache_k] * n), *([cache_v] * n), srev, cn_lane, knew, vnew, bias_new, z)


def _s5prep_kernel(lr_ref, li_ref, ldt_ref, lrx_ref, lix_ref, bre_ref, bim_ref,
                   abr_ref, abi_ref, bbr_ref, bbi_ref):
    dt = jnp.exp(ldt_ref[...])

    def discretise(lr, li):
        mag = jnp.exp(lr * dt)
        abr = mag * jnp.cos(li * dt)
        abi = mag * jnp.sin(li * dt)
        den = lr * lr + li * li
        pr = abr - 1.0
        return abr, abi, (pr * lr + abi * li) / den, (abi * lr - pr * li) / den

    abr, abi, _, _ = discretise(lr_ref[...], li_ref[...])
    abr_ref[...] = abr
    abi_ref[...] = abi
    _, _, fr, fi = discretise(lrx_ref[...], lix_ref[...])
    bre, bim = bre_ref[...], bim_ref[...]
    bbr_ref[...] = fr * bre - fi * bim
    bbi_ref[...] = fr * bim + fi * bre


def _s5prep_all(lam_re, lam_im, log_dt, b_re, b_im):
    depth, g, n = lam_re.shape
    c = b_re.shape[-1]
    small = pl.BlockSpec((None, g, n), lambda l: (l, 0, 0))
    wide = pl.BlockSpec((None, g, n * c), lambda l: (l, 0, 0))
    return pl.pallas_call(
        _s5prep_kernel,
        grid=(depth,),
        in_specs=[small, small, pl.BlockSpec((None, g, 1), lambda l: (l, 0, 0)), wide, wide, wide, wide],
        out_specs=[small, small, wide, wide],
        out_shape=[jax.ShapeDtypeStruct((depth, g, n), F32)] * 2 + [jax.ShapeDtypeStruct((depth, g, n * c), F32)] * 2,
        compiler_params=_cparams(("arbitrary",), 32),
        name="s5prep",
    )(lam_re, lam_im, log_dt.reshape(depth, g, 1),
      jnp.repeat(lam_re, c, axis=-1), jnp.repeat(lam_im, c, axis=-1),
      b_re.reshape(depth, g, n * c), b_im.reshape(depth, g, n * c))


def _s5_kernel(x_ref, h0_ref, bset_ref, cset_ref, ar_ref, ai_ref, dsk_ref, wglu_ref, bglu_ref,
               y_ref, hfin_ref, u_ref, hst_ref):
    rows = x_ref.shape[0]
    nb = SUBLANES
    half = SET_H // 2

    @pl.when(pl.program_id(0) == 0)
    def _():
        hst_ref[...] = h0_ref[...]

    x = x_ref[...]
    xb = x.astype(BF16)
    for s in range(S5_SETS):
        u_ref[:, s * SET_H:(s + 1) * SET_H] = _dot(xb[:, s * SET_X:(s + 1) * SET_X], bset_ref[s])

    ar, ai = ar_ref[...], ai_ref[...]

    def step(t, h):
        r0 = pl.multiple_of(t * nb, nb)
        u = u_ref[pl.ds(r0, nb), :]
        out = []
        for s in range(S5_SETS):
            o = s * SET_H
            hr, hi = h[:, o:o + half], h[:, o + half:o + SET_H]
            a_r, a_i = ar[:, s * half:(s + 1) * half], ai[:, s * half:(s + 1) * half]
            out.append(a_r * hr - a_i * hi + u[:, o:o + half])
            out.append(a_r * hi + a_i * hr + u[:, o + half:o + SET_H])
        hn = jnp.concatenate(out, axis=1)
        u_ref[pl.ds(r0, nb), :] = hn
        return hn

    h = lax.fori_loop(0, rows // nb, step, hst_ref[...])
    hst_ref[...] = h
    hfin_ref[...] = h

    hb = u_ref[...].astype(BF16)
    y = jnp.concatenate([_dot(hb[:, s * SET_H:(s + 1) * SET_H], cset_ref[s]) for s in range(S5_SETS)], axis=1)
    y = y + dsk_ref[...] * x
    y = 0.5 * y * (1.0 + jnp.tanh(0.7978845608028654 * (y + 0.044715 * (y * y * y))))
    y_ref[...] = y * jax.nn.sigmoid(_dot(y.astype(BF16), wglu_ref[...]) + bglu_ref[...])


def _s5(x_tb, h0, bset, cset, ar, ai, d_skip, w_glu, b_glu, tt):
    rows = x_tb.shape[0]
    blk = tt * SUBLANES
    const2 = lambda shape: pl.BlockSpec(shape, lambda i: (0, 0))
    const3 = lambda shape: pl.BlockSpec(shape, lambda i: (0, 0, 0))
    return pl.pallas_call(
        _s5_kernel,
        grid=(rows // blk,),
        in_specs=[pl.BlockSpec((blk, D_SSM), lambda i: (i, 0)),
                  const2((SUBLANES, D_STATE)),
                  const3((S5_SETS, SET_X, SET_H)), const3((S5_SETS, SET_H, SET_X)),
                  const2((SUBLANES, D_STATE // 2)), const2((SUBLANES, D_STATE // 2)),
                  const2((1, D_SSM)), const2((D_SSM, D_SSM)), const2((1, D_SSM))],
        out_specs=[pl.BlockSpec((blk, D_SSM), lambda i: (i, 0)), const2((SUBLANES, D_STATE))],
        out_shape=[jax.ShapeDtypeStruct((rows, D_SSM), F32), jax.ShapeDtypeStruct((SUBLANES, D_STATE), F32)],
        scratch_shapes=[pltpu.VMEM((blk, D_STATE), F32), pltpu.VMEM((SUBLANES, D_STATE), F32)],
        compiler_params=_cparams(("arbitrary",), 48),
        name="s5",
    )(x_tb, h0, bset, cset, ar, ai, d_skip, w_glu, b_glu)


def _s5_operands(abr, abi, bbr, bbi, c_re, c_im):
    g, n, c = N_SSM_GROUPS, SSM_STATE, SSM_GROUP
    gs = g // S5_SETS
    eye = jnp.eye(gs, dtype=F32)

    def in_map(bb):
        bb = bb.reshape(S5_SETS, gs, n, c).transpose(0, 1, 3, 2)
        return (bb[:, :, :, None, :] * eye[None, :, None, :, None]).reshape(S5_SETS, gs * c, gs * n)

    def out_map(cc):
        cc = cc.reshape(S5_SETS, gs, c, n).transpose(0, 1, 3, 2)
        return (cc[:, :, :, None, :] * eye[None, :, None, :, None]).reshape(S5_SETS, gs * n, gs * c)

    bset = jnp.concatenate([in_map(bbr), in_map(bbi)], axis=2).astype(BF16)
    cset = jnp.concatenate([out_map(c_re), -out_map(c_im)], axis=1).astype(BF16)
    ar = jnp.broadcast_to(abr.reshape(1, g * n), (SUBLANES, g * n))
    ai = jnp.broadcast_to(abi.reshape(1, g * n), (SUBLANES, g * n))
    return bset, cset, ar, ai


def _state_to_lanes(h_re, h_im):
    nb = h_re.shape[0]
    gs = N_SSM_GROUPS // S5_SETS
    st = jnp.stack([h_re.reshape(nb, S5_SETS, gs, SSM_STATE), h_im.reshape(nb, S5_SETS, gs, SSM_STATE)], axis=2)
    return st.reshape(nb, D_STATE)


def _lanes_to_state(h):
    nb = h.shape[0]
    gs = N_SSM_GROUPS // S5_SETS
    st = h.reshape(nb, S5_SETS, 2, gs, SSM_STATE)
    return (st[:, :, 0].reshape(nb, N_SSM_GROUPS, SSM_STATE), st[:, :, 1].reshape(nb, N_SSM_GROUPS, SSM_STATE))


def _cmlp_kernel(lc, emit_vn, u_ref, v_ref, gm_ref, g_ref, b_ref, w_ref, bias_ref, o_ref, *vn_ref):
    tm = u_ref.shape[0]
    v = v_ref[...]
    mu = jnp.mean(v, axis=-1, keepdims=True)
    d = v - mu
    var = jnp.mean(d * d, axis=-1, keepdims=True)
    vn = d * lax.rsqrt(var + EPS) * g_ref[...] + b_ref[...]
    if emit_vn:
        vn_ref[0][...] = vn
    row = lax.broadcasted_iota(jnp.int32, (lc, lc), 0)
    col = lax.broadcasted_iota(jnp.int32, (lc, lc), 1)
    ug = u_ref[...] * _silu(gm_ref[...])
    vb = vn.astype(BF16)
    for g in range(N_CMLP_GROUPS):
        wg = jnp.where(col <= row, w_ref[g], 0.0).astype(BF16)
        lanes = slice(g * CMLP_GROUP, (g + 1) * CMLP_GROUP)
        for c in range(tm // lc):
            rws = slice(c * lc, (c + 1) * lc)
            zc = _dot(wg, vb[rws, lanes]) + bias_ref[:, lanes]
            o_ref[rws, lanes] = (ug[rws, lanes] * zc).astype(BF16)


def _cmlp(z, ln_g, ln_b, w, bias, tm, lc, emit_vn):
    t = z.shape[0]
    col = lambda c: pl.BlockSpec((tm, D_CMLP), lambda i: (i, c))
    const2 = lambda shape: pl.BlockSpec(shape, lambda i: (0, 0))
    out_specs = [pl.BlockSpec((tm, D_CMLP), lambda i: (i, 0))]
    out_shape = [jax.ShapeDtypeStruct((t, D_CMLP), BF16)]
    if emit_vn:
        out_specs.append(pl.BlockSpec((tm, D_CMLP), lambda i: (i, 0)))
        out_shape.append(jax.ShapeDtypeStruct((t, D_CMLP), F32))
    return pl.pallas_call(
        functools.partial(_cmlp_kernel, lc, emit_vn),
        grid=(t // tm,),
        in_specs=[col(COL_U // D_CMLP), col(COL_VV // D_CMLP), col(COL_GM // D_CMLP),
                  const2((1, D_CMLP)), const2((1, D_CMLP)),
                  pl.BlockSpec((N_CMLP_GROUPS, lc, lc), lambda i: (0, 0, 0)),
                  const2((lc, D_CMLP))],
        out_specs=out_specs,
        out_shape=out_shape,
        compiler_params=_cparams(("arbitrary",), 32),
        name="cmlp",
    )(z, z, z, ln_g, ln_b, w, bias)


def _outproj_kernel(x_ref, a_ref, s_ref, gs_ref, m_ref, gate_ref, w_ref, o_ref):
    sg = (s_ref[...] * _silu(gs_ref[...])).astype(BF16)
    acc = (_dot(a_ref[...].astype(BF16), w_ref[0:D_ATTN, :])
           + _dot(sg, w_ref[D_ATTN:D_ATTN + D_SSM, :])
           + _dot(m_ref[...], w_ref[D_ATTN + D_SSM:, :]))
    o_ref[...] = x_ref[...] + gate_ref[...] * acc


def _outproj(x, a, s, z, m, gate, w_out, tm, rows_per_mod):
    t = x.shape[0]
    mod_rows = gate.shape[1]
    return pl.pallas_call(
        _outproj_kernel,
        grid=(t // tm,),
        in_specs=[pl.BlockSpec((tm, D_MODEL), lambda i: (i, 0)),
                  pl.BlockSpec((tm, D_ATTN), lambda i: (i, 0)),
                  pl.BlockSpec((tm, D_SSM), lambda i: (i, 0)),
                  pl.BlockSpec((tm, D_SSM), lambda i: (i, COL_GS // D_SSM)),
                  pl.BlockSpec((tm, D_CMLP), lambda i: (i, 0)),
                  pl.BlockSpec((None, mod_rows, D_MODEL), lambda i: ((i * tm) // rows_per_mod, 0, 0)),
                  pl.BlockSpec((D_MODEL, D_MODEL), lambda i: (0, 0))],
        out_specs=pl.BlockSpec((tm, D_MODEL), lambda i: (i, 0)),
        out_shape=jax.ShapeDtypeStruct((t, D_MODEL), F32),
        compiler_params=_cparams(("arbitrary",), 56),
        name="outproj",
    )(x, a, s, z, m, gate, w_out)


def _pack_w_in(w_in):
    d_q = 3 * D_ATTN
    fg = w_in[:, :, d_q:d_q + N_HEADS]
    main = jnp.concatenate([w_in[:, :, :d_q], w_in[:, :, d_q + N_HEADS:]], axis=-1).astype(BF16)
    w_fg = jnp.pad(fg, ((0, 0), (0, 0), (0, LANES - N_HEADS))).astype(BF16)
    return main, w_fg


def _layer(x, mods, attn_fn, h0_lanes, seq_len, batch, wl, consts, tm, tt, lc, emit_vn, emit_vt):
    t = x.shape[0]
    shift, scale, gate, rows_per_mod = mods
    z, logf = _inproj(x, wl["norm_g"], scale, shift, wl["w_main"], wl["w_fg"], wl["b_f"], tm, rows_per_mod)
    qa, ka, kn, vx, f = _qkprep(z, logf, wl["qg"], wl["kg"], consts, min(tm, TQ), seq_len, emit_vt)
    a_out = attn_fn(qa, ka, vx, f, z)

    xs = z[:, COL_XS:COL_XS + D_SSM].reshape(batch, seq_len, D_SSM).transpose(1, 0, 2)
    xs = jnp.pad(xs, ((0, 0), (0, SUBLANES - batch), (0, 0))).reshape(seq_len * SUBLANES, D_SSM)
    s_tb, hfin = _s5(xs, h0_lanes, wl["bset"], wl["cset"], wl["ar"], wl["ai"], wl["d_skip"],
                     wl["w_glu"], wl["b_glu"], tt)
    s_pre = s_tb.reshape(seq_len, SUBLANES, D_SSM)[:, :batch].transpose(1, 0, 2).reshape(t, D_SSM)

    m_res = _cmlp(z, wl["ln_g"], wl["ln_b"], wl["w_s"], wl["b_s"], min(tm, 512), lc, emit_vn)
    y = _outproj(x, a_out, s_pre, z, m_res[0], gate, wl["w_out"], min(tm, 512), rows_per_mod)
    h_re, h_im = _lanes_to_state(hfin[:batch])
    v = z[:, COL_V:COL_V + D_ATTN]
    return y, (kn, v, logf[:, :N_HEADS], h_re, h_im, m_res[1] if emit_vn else None)


def kernel(x_prompt, x_sample, c_prompt, c_sample, cache_k, cache_v, cache_logf, state_ssm_re, state_ssm_im,
           page_table, norm_g, w_ada, b_ada, w_in, b_f, q_norm_g, k_norm_g, lam_re, lam_im, log_dt,
           b_re, b_im, c_re, c_im, d_skip, w_glu, b_glu, sgu_ln_g, sgu_ln_b, w_s, b_s, w_out):
    depth = w_in.shape[0]
    bp, seq, _ = x_prompt.shape
    bd, t_dec, _ = x_sample.shape
    n_dec, n_pages = page_table.shape

    c_all = jnp.concatenate([c_prompt, c_sample], axis=0)
    c_all = jnp.pad(c_all, ((0, 2 * SUBLANES - bp - bd), (0, 0)))
    mod = _ada_all(c_all, w_ada, b_ada)

    w_main, w_fg = _pack_w_in(w_in)
    w_out_b = w_out.astype(BF16)
    w_glu_b = w_glu.astype(BF16)
    b_f_pad = jnp.pad(b_f, ((0, 0), (0, LANES - N_HEADS))).reshape(depth, 1, LANES)
    consts = _aug_placement()
    abr, abi, bbr, bbi = _s5prep_all(lam_re, lam_im, log_dt, b_re, b_im)
    srev = _srev_all(cache_logf, page_table, t_dec)

    eye_d = jnp.eye(bd, dtype=F32)
    w_s_dec = (eye_d[None, None, :, None, :, None] * w_s[:, :, None, :t_dec, None, :t_dec]).reshape(
        depth, N_CMLP_GROUPS, bd * t_dec, bd * t_dec)
    bias_p = jnp.repeat(b_s.transpose(0, 2, 1), CMLP_GROUP, axis=-1)
    bias_d = jnp.tile(bias_p[:, :t_dec], (1, bd, 1))

    yp = x_prompt.reshape(bp * seq, D_MODEL)
    ys = x_sample.reshape(bd * t_dec, D_MODEL)
    h_zero = jnp.zeros((SUBLANES, D_STATE), F32)
    outs_p, outs_s = [], []
    for l in range(depth):
        bset, cset, ar, ai = _s5_operands(abr[l], abi[l], bbr[l], bbi[l], c_re[l], c_im[l])
        wl = dict(norm_g=norm_g[l][None], w_main=w_main[l], w_fg=w_fg[l], b_f=b_f_pad[l],
                  qg=jnp.tile(q_norm_g[l], N_HEADS)[None], kg=jnp.tile(k_norm_g[l], N_HEADS)[None],
                  bset=bset, cset=cset, ar=ar, ai=ai, d_skip=d_skip[l][None], w_glu=w_glu_b[l],
                  b_glu=b_glu[l][None], ln_g=sgu_ln_g[l][None], ln_b=sgu_ln_b[l][None], w_out=w_out_b[l])
        shift, scale, gate = (mod[l][:, i * D_MODEL:(i + 1) * D_MODEL] for i in range(3))

        mods_p = (shift[:bp, None], scale[:bp, None], gate[:bp, None], seq)
        attn_p = lambda qa, ka, vt, f, z: _fox_prompt(qa, ka, vt, z, bp, seq, TQ)
        yp, cache_p = _layer(yp, mods_p, attn_p, h_zero, seq, bp, dict(wl, w_s=w_s[l], b_s=bias_p[l]),
                             consts, 1024, 64, CHUNK, False, True)
        outs_p.append(cache_p)

        rep = lambda m: jnp.repeat(m[bp:bp + bd], t_dec, axis=0)[None]
        mods_s = (rep(shift), rep(scale), rep(gate), bd * t_dec)

        def attn_s(qa, ka, vb, f, z, l=l):
            lanes = N_HEADS * t_dec
            q = qa.reshape(bd, t_dec, N_HEADS, 2 * HEAD_DIM)[..., :HEAD_DIM]
            w = q.transpose(0, 3, 2, 1).reshape(bd, HEAD_DIM, lanes)
            knew = ka.reshape(bd, t_dec, N_HEADS, 2 * HEAD_DIM)[..., :HEAD_DIM].reshape(bd, t_dec * N_HEADS, HEAD_DIM)
            vnew = vb.reshape(bd, t_dec * N_HEADS, HEAD_DIM)
            cn = f[:, :N_HEADS].reshape(bd, t_dec, N_HEADS)
            cn_lane = cn.transpose(0, 2, 1).reshape(bd, 1, lanes)
            cn_keys = jnp.broadcast_to(cn[:, :, :, None], (bd, t_dec, N_HEADS, t_dec)).reshape(bd, t_dec, lanes)
            bias_new = jnp.repeat(cn_lane - cn_keys, N_HEADS, axis=1)
            return _fox_sample(l, page_table, w, cache_k, cache_v, srev, cn_lane, knew, vnew, bias_new, z)

        h0 = _state_to_lanes(state_ssm_re[:, l], state_ssm_im[:, l])
        ys, cache_s = _layer(ys, mods_s, attn_s, h0, t_dec, bd, dict(wl, w_s=w_s_dec[l], b_s=bias_d[l]),
                             consts, bd * t_dec, t_dec, bd * t_dec, True, False)
        outs_s.append(cache_s)

    def stack(outs, idx, shape):
        return jnp.stack([o[idx].reshape(shape) for o in outs], axis=1)

    return (yp.reshape(bp, seq, D_MODEL), ys.reshape(bd, t_dec, D_MODEL),
            stack(outs_p, 0, (bp, seq, N_HEADS, HEAD_DIM)), stack(outs_p, 1, (bp, seq, N_HEADS, HEAD_DIM)),
            stack(outs_p, 2, (bp, seq, N_HEADS)),
            stack(outs_p, 3, (bp, N_SSM_GROUPS, SSM_STATE)), stack(outs_p, 4, (bp, N_SSM_GROUPS, SSM_STATE)),
            stack(outs_s, 0, (bd, t_dec, N_HEADS, HEAD_DIM)), stack(outs_s, 1, (bd, t_dec, N_HEADS, HEAD_DIM)),
            stack(outs_s, 2, (bd, t_dec, N_HEADS)),
            stack(outs_s, 3, (bd, N_SSM_GROUPS, SSM_STATE)), stack(outs_s, 4, (bd, N_SSM_GROUPS, SSM_STATE)),
            stack(outs_s, 5, (bd, t_dec, D_CMLP)))
```

```python
import functools

import jax
import jax.numpy as jnp
from jax import lax
from jax.experimental import pallas as pl
from jax.experimental.pallas import tpu as pltpu

F32 = jnp.float32
BF16 = jnp.bfloat16

D_MODEL = 2048
HEAD_DIM = 64
D_ATTN = 1024
N_HEADS = 16
D_SSM = 512
SSM_GROUP = 16
N_SSM_GROUPS = 32
SSM_STATE = 64
D_CMLP = 512
CHUNK = 128
N_CMLP_GROUPS = 4
CMLP_GROUP = 128
PAGE_SIZE = 128
EPS = 1e-6
LANES = 128
SUBLANES = 8
NEG_BIG = -1e30
LOG2E = 1.4426950408889634

COL_Q, COL_K, COL_V, COL_GA = 0, 1024, 2048, 3072
COL_XS, COL_GS, COL_U, COL_VV, COL_GM = 4096, 4608, 5120, 5632, 6144
D_Z = 6656
D_STATE = 2 * N_SSM_GROUPS * SSM_STATE
S5_SETS = 2
SET_X = D_SSM // S5_SETS
SET_H = D_STATE // S5_SETS
PAGES_PER_STEP = 8
SREV_PAGES_PER_STEP = 16
TQ = 512


def _cparams(sem, vmem_mb):
    return pltpu.CompilerParams(dimension_semantics=sem, vmem_limit_bytes=vmem_mb * 1024 * 1024)


def _split3(x):
    hi = x.astype(BF16)
    r1 = x - hi.astype(F32)
    mid = r1.astype(BF16)
    lo = (r1 - mid.astype(F32)).astype(BF16)
    return hi, mid, lo


def _dot(a, b):
    return jnp.dot(a, b, preferred_element_type=F32)


def _dot_nt(a, b):
    return lax.dot_general(a, b, (((1,), (1,)), ((), ())), preferred_element_type=F32)


def _silu(x):
    return x * jax.nn.sigmoid(x)


def _ada_kernel(c_ref, w_ref, b_ref, o_ref):
    s = _silu(c_ref[...]).astype(BF16)
    o_ref[...] = _dot(s, w_ref[...].astype(BF16)) + b_ref[...]


def _ada_all(c_all, w_ada, b_ada):
    depth = w_ada.shape[0]
    rows = c_all.shape[0]
    tn = 1024
    return pl.pallas_call(
        _ada_kernel,
        grid=(depth, 3 * D_MODEL // tn),
        in_specs=[
            pl.BlockSpec((rows, D_MODEL), lambda l, j: (0, 0)),
            pl.BlockSpec((None, D_MODEL, tn), lambda l, j: (l, 0, j)),
            pl.BlockSpec((None, 1, tn), lambda l, j: (l, 0, j)),
        ],
        out_specs=pl.BlockSpec((None, rows, tn), lambda l, j: (l, 0, j)),
        out_shape=jax.ShapeDtypeStruct((depth, rows, 3 * D_MODEL), F32),
        compiler_params=_cparams(("arbitrary", "arbitrary"), 48),
        name="ada",
    )(c_all, w_ada, b_ada.reshape(depth, 1, 3 * D_MODEL))


def _inproj_kernel(x_ref, g_ref, sc_ref, sh_ref, wt_ref, wfgt_ref, bf_ref, z_ref, lf_ref, h_ref):
    @pl.when(pl.program_id(1) == 0)
    def _():
        x = x_ref[...]
        ms = jnp.mean(x * x, axis=-1, keepdims=True)
        y = x * lax.rsqrt(ms + EPS) * g_ref[...]
        hb = (y * (1.0 + sc_ref[...]) + sh_ref[...]).astype(BF16)
        h_ref[...] = hb
        fg = _dot_nt(hb, wfgt_ref[...]) + bf_ref[...]
        lf_ref[...] = jnp.minimum(fg, 0.0) - jnp.log1p(jnp.exp(-jnp.abs(fg)))

    z_ref[...] = _dot_nt(h_ref[...], wt_ref[...])


def _inproj(x, norm_g, scale, shift, wt_main, wt_fg, b_f, tm, rows_per_mod):
    t = x.shape[0]
    tn = 512
    mod_rows = scale.shape[1]
    mod_spec = pl.BlockSpec((None, mod_rows, D_MODEL), lambda i, j: ((i * tm) // rows_per_mod, 0, 0))
    return pl.pallas_call(
        _inproj_kernel,
        grid=(t // tm, D_Z // tn),
        in_specs=[
            pl.BlockSpec((tm, D_MODEL), lambda i, j: (i, 0)),
            pl.BlockSpec((1, D_MODEL), lambda i, j: (0, 0)),
            mod_spec,
            mod_spec,
            pl.BlockSpec((tn, D_MODEL), lambda i, j: (j, 0)),
            pl.BlockSpec((LANES, D_MODEL), lambda i, j: (0, 0)),
            pl.BlockSpec((1, LANES), lambda i, j: (0, 0)),
        ],
        out_specs=[
            pl.BlockSpec((tm, tn), lambda i, j: (i, j)),
            pl.BlockSpec((tm, LANES), lambda i, j: (i, 0)),
        ],
        out_shape=[jax.ShapeDtypeStruct((t, D_Z), F32), jax.ShapeDtypeStruct((t, LANES), F32)],
        scratch_shapes=[pltpu.VMEM((tm, D_MODEL), BF16)],
        compiler_params=_cparams(("arbitrary", "arbitrary"), 48),
        name="inproj",
    )(x, norm_g, scale, shift, wt_main, wt_fg, b_f)


def _qkprep_kernel(seq_len, prompt, q_ref, k_ref, v_ref, lf_ref, qg_ref, kg_ref, seg_ref, pq_ref, pk_ref,
                   qa_ref, ka_ref, *rest):
    if prompt:
        knt_ref, vt32_ref, vt16_ref, lft_ref, f_ref, carry_ref = rest
    else:
        kn_ref, vb_ref, f_ref, carry_ref = rest
    tm = q_ref.shape[0]
    pairs = N_HEADS // 2
    row = lax.broadcasted_iota(jnp.int32, (tm, tm), 0)
    col = lax.broadcasted_iota(jnp.int32, (tm, tm), 1)
    tri = col <= row
    if seq_len < tm:
        shift = seq_len.bit_length() - 1
        tri = tri & (lax.shift_right_logical(row, shift) == lax.shift_right_logical(col, shift))
    tri_b = jnp.where(tri, 1.0, 0.0).astype(BF16)
    lf = lf_ref[...]
    hi, mid, lo = _split3(lf)
    f = _dot(tri_b, hi) + _dot(tri_b, mid) + _dot(tri_b, lo)
    if seq_len > tm:
        @pl.when(pl.program_id(0) % (seq_len // tm) == 0)
        def _():
            carry_ref[...] = jnp.zeros_like(carry_ref)

        f = f + carry_ref[...]
        carry_ref[...] = f[tm - 1:tm, :]
    f_ref[...] = f

    def head_norm(x, g):
        ss = _dot((x * x).astype(BF16), seg_ref[...])
        return x * lax.rsqrt(ss * (1.0 / HEAD_DIM) + EPS) * g

    qs = head_norm(q_ref[...], qg_ref[...]) * (HEAD_DIM ** -0.5 * LOG2E)
    kn = head_norm(k_ref[...], kg_ref[...])
    v = v_ref[...]
    if prompt:
        lft_ref[...] = lf.T[0:N_HEADS, :]
        for p in range(pairs):
            cols = slice(p * LANES, (p + 1) * LANES)
            knt_ref[cols, :] = kn[:, cols].T
            vt = v[:, cols].T
            vt32_ref[cols, :] = vt
            vt16_ref[p] = vt.astype(BF16)
    else:
        kn_ref[...] = kn
        vb_ref[...] = v.astype(BF16)

    fh, fm, fl = _split3(f * LOG2E)
    lane = lax.broadcasted_iota(jnp.int32, (tm, LANES), 1)
    parts = jnp.where(lane < 16, fh.astype(F32),
                      jnp.where(lane < 32, pltpu.roll(fm.astype(F32), 16, 1),
                                jnp.where(lane < 48, pltpu.roll(fl.astype(F32), 32, 1),
                                          jnp.where(lane == 48, 1.0, 0.0)))).astype(BF16)
    ext_q = _dot(parts, pq_ref[...])
    ext_k = _dot(parts, pk_ref[...])
    low = lane < HEAD_DIM
    for src, ext, dst in ((qs, ext_q, qa_ref), (kn, ext_k, ka_ref)):
        for p in range(pairs):
            blk = src[:, p * LANES:(p + 1) * LANES]
            e = ext[:, p * LANES:(p + 1) * LANES]
            dst[:, (2 * p) * LANES:(2 * p + 1) * LANES] = jnp.where(low, blk, pltpu.roll(e, 64, 1)).astype(BF16)
            dst[:, (2 * p + 1) * LANES:(2 * p + 2) * LANES] = jnp.where(low, pltpu.roll(blk, 64, 1), e).astype(BF16)


def _aug_placement():
    r = jnp.arange(LANES)[:, None]
    c = jnp.arange(N_HEADS * HEAD_DIM)[None, :]
    rp, rh = r // 16, r % 16
    ch, cc = c // HEAD_DIM, c % HEAD_DIM
    f_rows = (r < 48) & (rh == ch)
    one_row = r == 48
    pq = jnp.where(f_rows & (cc == rp), 1.0, 0.0) + jnp.where(one_row & (cc >= 3) & (cc < 6), 1.0, 0.0)
    pk = jnp.where(f_rows & (cc == rp + 3), -1.0, 0.0) + jnp.where(one_row & (cc < 3), 1.0, 0.0)
    seg = jnp.where(jnp.arange(D_ATTN)[:, None] // HEAD_DIM == jnp.arange(D_ATTN)[None, :] // HEAD_DIM, 1.0, 0.0)
    return pq.astype(BF16), pk.astype(BF16), seg.astype(BF16)


def _qkprep(z, logf, qg, kg, consts, tm, seq_len, prompt):
    t = z.shape[0]
    pq, pk, seg = consts
    col = lambda c: pl.BlockSpec((tm, D_ATTN), lambda i: (i, c))
    const = lambda shape: pl.BlockSpec(shape, lambda i: (0, 0))
    rows = lambda width: pl.BlockSpec((tm, width), lambda i: (i, 0))
    pairs = N_HEADS // 2
    out_specs = [rows(2 * D_ATTN), rows(2 * D_ATTN)]
    out_shape = [jax.ShapeDtypeStruct((t, 2 * D_ATTN), BF16), jax.ShapeDtypeStruct((t, 2 * D_ATTN), BF16)]
    if prompt:
        nk = seq_len // tm
        batch = t // seq_len
        tspec = lambda width: pl.BlockSpec((None, width, tm), lambda i: (i // nk, 0, i % nk))
        out_specs += [tspec(D_ATTN), tspec(D_ATTN),
                      pl.BlockSpec((None, pairs, None, LANES, tm), lambda i: (i // nk, 0, i % nk, 0, 0)),
                      tspec(N_HEADS)]
        out_shape += [jax.ShapeDtypeStruct((batch, D_ATTN, seq_len), F32),
                      jax.ShapeDtypeStruct((batch, D_ATTN, seq_len), F32),
                      jax.ShapeDtypeStruct((batch, pairs, nk, LANES, tm), BF16),
                      jax.ShapeDtypeStruct((batch, N_HEADS, seq_len), F32)]
    else:
        out_specs += [rows(D_ATTN), rows(D_ATTN)]
        out_shape += [jax.ShapeDtypeStruct((t, D_ATTN), F32), jax.ShapeDtypeStruct((t, D_ATTN), BF16)]
    out_specs.append(rows(LANES))
    out_shape.append(jax.ShapeDtypeStruct((t, LANES), F32))
    return pl.pallas_call(
        functools.partial(_qkprep_kernel, seq_len, prompt),
        grid=(t // tm,),
        in_specs=[col(COL_Q // D_ATTN), col(COL_K // D_ATTN), col(COL_V // D_ATTN), rows(LANES),
                  const((1, D_ATTN)), const((1, D_ATTN)), const((D_ATTN, D_ATTN)),
                  const((LANES, D_ATTN)), const((LANES, D_ATTN))],
        out_specs=out_specs,
        out_shape=out_shape,
        scratch_shapes=[pltpu.VMEM((1, LANES), F32)],
        compiler_params=_cparams(("arbitrary",), 56),
        name="qkprep",
    )(z, z, z, logf, qg, kg, seg, pq, pk)


def _softmax_absorb(s, vt, m_ref, l_ref, acc_ref, idx):
    m_prev = m_ref[idx]
    m_new = jnp.maximum(m_prev, jnp.max(s, axis=0, keepdims=True))
    alpha = jnp.exp2(m_prev - m_new)
    p = jnp.exp2(s - m_new)
    l_ref[idx] = alpha * l_ref[idx] + jnp.sum(p, axis=0, keepdims=True)
    acc_ref[idx] = alpha * acc_ref[idx] + _dot(vt, p.astype(BF16))
    m_ref[idx] = m_new


def _fox_kernel(qa_ref, ka_ref, vt_ref, ga_ref, o_ref, m_ref, l_ref, acc_ref):
    i = pl.program_id(2)
    tq = qa_ref.shape[0]
    tk = vt_ref.shape[2]
    m_ref[...] = jnp.full_like(m_ref, NEG_BIG)
    l_ref[...] = jnp.zeros_like(l_ref)
    acc_ref[...] = jnp.zeros_like(acc_ref)

    def tile(j, masked):
        k0 = pl.multiple_of(j * tk, tk)
        for hh in range(2):
            k = ka_ref[pl.ds(k0, tk), hh * LANES:(hh + 1) * LANES]
            s = _dot_nt(k, qa_ref[:, hh * LANES:(hh + 1) * LANES])
            if masked:
                row = lax.broadcasted_iota(jnp.int32, (tk, tq), 0)
                col = lax.broadcasted_iota(jnp.int32, (tk, tq), 1)
                s = jnp.where(row <= col, s, NEG_BIG)
            _softmax_absorb(s, vt_ref[j, hh * HEAD_DIM:(hh + 1) * HEAD_DIM, :], m_ref, l_ref, acc_ref, hh)

    def body(j, carry):
        tile(j, False)
        return carry

    lax.fori_loop(0, i, body, 0)
    tile(i, True)
    o_t = jnp.concatenate([acc_ref[0] / l_ref[0], acc_ref[1] / l_ref[1]], axis=0)
    o_ref[...] = (o_t.T * _silu(ga_ref[...])).astype(BF16)


def _fox_prompt(qa, ka, vt, z, batch, seq, tq):
    nq = seq // tq
    pairs = N_HEADS // 2
    ga_blk = COL_GA // LANES
    return pl.pallas_call(
        _fox_kernel,
        grid=(batch, pairs, nq),
        in_specs=[
            pl.BlockSpec((tq, 2 * LANES), lambda b, p, i: (b * nq + i, p)),
            pl.BlockSpec((seq, 2 * LANES), lambda b, p, i: (b, p)),
            pl.BlockSpec((None, None, nq, LANES, tq), lambda b, p, i: (b, p, 0, 0, 0)),
            pl.BlockSpec((tq, LANES), lambda b, p, i: (b * nq + i, ga_blk + p)),
        ],
        out_specs=pl.BlockSpec((tq, LANES), lambda b, p, i: (b * nq + i, p)),
        out_shape=jax.ShapeDtypeStruct((batch * seq, D_ATTN), BF16),
        scratch_shapes=[pltpu.VMEM((2, 1, tq), F32), pltpu.VMEM((2, 1, tq), F32),
                        pltpu.VMEM((2, HEAD_DIM, tq), F32)],
        compiler_params=_cparams(("arbitrary",) * 3, 48),
        name="fox",
    )(qa, ka, vt, z)


def _srev_kernel(pt_ref, *refs):
    n = SREV_PAGES_PER_STEP
    lf_refs = refs[:n]
    out_ref, carry_ref = refs[n:]

    @pl.when(pl.program_id(2) == 0)
    def _():
        carry_ref[...] = jnp.zeros_like(carry_ref)

    row = lax.broadcasted_iota(jnp.int32, (PAGE_SIZE, PAGE_SIZE), 0)
    col = lax.broadcasted_iota(jnp.int32, (PAGE_SIZE, PAGE_SIZE), 1)
    later = jnp.where(row > col, 1.0, 0.0).astype(BF16)
    for i in reversed(range(n)):
        x = lf_refs[i][...]
        hi, mid, lo = _split3(x)
        out_ref[:, i * PAGE_SIZE:(i + 1) * PAGE_SIZE] = (
            _dot(hi, later) + _dot(mid, later) + _dot(lo, later) + carry_ref[...])
        carry_ref[...] = carry_ref[...] + jnp.sum(x, axis=1, keepdims=True)


def _srev_all(logf_t, page_table):
    depth = logf_t.shape[1]
    n_dec, n_pages = page_table.shape
    n = SREV_PAGES_PER_STEP
    nchunk = n_pages // n

    def page_spec(i):
        return pl.BlockSpec(
            (None, None, N_HEADS, PAGE_SIZE),
            lambda l, b, c, pt: (pt[b, (nchunk - 1 - c) * n + i], l, 0, 0))

    grid_spec = pltpu.PrefetchScalarGridSpec(
        num_scalar_prefetch=1,
        grid=(depth, n_dec, nchunk),
        in_specs=[page_spec(i) for i in range(n)],
        out_specs=pl.BlockSpec((None, None, N_HEADS, n * PAGE_SIZE), lambda l, b, c, pt: (l, b, 0, nchunk - 1 - c)),
        scratch_shapes=[pltpu.VMEM((N_HEADS, LANES), F32)],
    )
    return pl.pallas_call(
        _srev_kernel,
        grid_spec=grid_spec,
        out_shape=jax.ShapeDtypeStruct((depth, n_dec, N_HEADS, n_pages * PAGE_SIZE), F32),
        compiler_params=_cparams(("arbitrary",) * 3, 32),
        name="srev",
    )(page_table, *([logf_t] * n))


def _foxdec_kernel(pt_ref, *refs):
    n = PAGES_PER_STEP
    q_ref = refs[0]
    kt_refs = refs[1:1 + n]
    vt_refs = refs[1 + n:1 + 2 * n]
    srev_ref, knew_ref, vnew_ref, cncol_ref, cnrow_ref, ga_ref, o_ref, m_ref, l_ref, acc_ref = refs[1 + 2 * n:]
    c = pl.program_id(1)
    rows = q_ref.shape[0]
    npos = n * PAGE_SIZE

    @pl.when(c == 0)
    def _():
        m_ref[...] = jnp.full_like(m_ref, NEG_BIG)
        l_ref[...] = jnp.zeros_like(l_ref)
        acc_ref[...] = jnp.zeros_like(acc_ref)

    def accumulate(s, pv):
        m_prev = m_ref[...]
        m_new = jnp.maximum(m_prev, jnp.max(s, axis=1, keepdims=True))
        alpha = jnp.exp2(m_prev - m_new)
        p = jnp.exp2(s - m_new)
        l_ref[...] = alpha * l_ref[...] + jnp.sum(p, axis=1, keepdims=True)
        acc_ref[...] = alpha * acc_ref[...] + pv(p.astype(BF16))
        m_ref[...] = m_new

    q = q_ref[...]
    cn_col = cncol_ref[...] * LOG2E
    kt = jnp.concatenate([r[...].astype(BF16) for r in kt_refs], axis=1)
    vt = jnp.concatenate([r[...].astype(BF16) for r in vt_refs], axis=1)
    s = _dot(q, kt)
    s = (s.reshape(rows // N_HEADS, N_HEADS, npos) + (srev_ref[...] * LOG2E)[None]).reshape(rows, npos)
    s = s + jnp.concatenate([cn_col] * n, axis=1)
    accumulate(s, lambda p: _dot_nt(p, vt))

    @pl.when(c == pl.num_programs(1) - 1)
    def _():
        sn = _dot_nt(q, knew_ref[...]) + (cn_col - cnrow_ref[...] * LOG2E)
        row = lax.broadcasted_iota(jnp.int32, (rows, LANES), 0)
        col = lax.broadcasted_iota(jnp.int32, (rows, LANES), 1)
        sn = jnp.where(col <= lax.shift_right_logical(row, 4), sn, NEG_BIG)
        accumulate(sn, lambda p: _dot(p, vnew_ref[...]))
        o = acc_ref[...] / l_ref[...]
        row = lax.broadcasted_iota(jnp.int32, (rows, D_ATTN), 0)
        col = lax.broadcasted_iota(jnp.int32, (rows, D_ATTN), 1)
        own = (row & (N_HEADS - 1)) == lax.shift_right_logical(col, 6)
        o = jnp.sum(jnp.where(own, o, 0.0).reshape(rows // N_HEADS, N_HEADS, D_ATTN), axis=1)
        o_ref[...] = o * _silu(ga_ref[...])


def _fox_sample(layer, page_table, qbd, kt_pages, vt_pages, srev, knew, vnew, cn_col, cn_row, z):
    n_dec, n_pages = page_table.shape
    n = PAGES_PER_STEP
    nchunk = n_pages // n
    rows = qbd.shape[1]
    t_len = rows // N_HEADS

    def page_spec(i):
        return pl.BlockSpec((None, None, D_ATTN, PAGE_SIZE),
                            lambda b, c, pt: (pt[b, c * n + i], layer, 0, 0))

    per_b = lambda shape: pl.BlockSpec((None,) + shape, lambda b, c, pt: (b, 0, 0))
    grid_spec = pltpu.PrefetchScalarGridSpec(
        num_scalar_prefetch=1,
        grid=(n_dec, nchunk),
        in_specs=([per_b((rows, D_ATTN))]
                  + [page_spec(i) for i in range(n)]
                  + [page_spec(i) for i in range(n)]
                  + [pl.BlockSpec((None, None, N_HEADS, n * PAGE_SIZE), lambda b, c, pt: (layer, b, 0, c)),
                     per_b((LANES, D_ATTN)), per_b((LANES, D_ATTN)),
                     per_b((rows, LANES)), per_b((rows, LANES)),
                     pl.BlockSpec((t_len, D_ATTN), lambda b, c, pt: (b, COL_GA // D_ATTN))]),
        out_specs=pl.BlockSpec((t_len, D_ATTN), lambda b, c, pt: (b, 0)),
        scratch_shapes=[pltpu.VMEM((rows, 1), F32), pltpu.VMEM((rows, 1), F32),
                        pltpu.VMEM((rows, D_ATTN), F32)],
    )
    return pl.pallas_call(
        _foxdec_kernel,
        grid_spec=grid_spec,
        out_shape=jax.ShapeDtypeStruct((n_dec * t_len, D_ATTN), F32),
        compiler_params=_cparams(("arbitrary", "arbitrary"), 56),
        name="foxdec",
    )(page_table, qbd, *([kt_pages] * n), *([vt_pages] * n), srev, knew, vnew, cn_col, cn_row, z)


def _s5prep_kernel(lr_ref, li_ref, ldt_ref, lrx_ref, lix_ref, bre_ref, bim_ref,
                   abr_ref, abi_ref, bbr_ref, bbi_ref):
    dt = jnp.exp(ldt_ref[...])

    def discretise(lr, li):
        mag = jnp.exp(lr * dt)
        abr = mag * jnp.cos(li * dt)
        abi = mag * jnp.sin(li * dt)
        den = lr * lr + li * li
        pr = abr - 1.0
        return abr, abi, (pr * lr + abi * li) / den, (abi * lr - pr * li) / den

    abr, abi, _, _ = discretise(lr_ref[...], li_ref[...])
    abr_ref[...] = abr
    abi_ref[...] = abi
    _, _, fr, fi = discretise(lrx_ref[...], lix_ref[...])
    bre, bim = bre_ref[...], bim_ref[...]
    bbr_ref[...] = fr * bre - fi * bim
    bbi_ref[...] = fr * bim + fi * bre


def _s5prep_all(lam_re, lam_im, log_dt, b_re, b_im):
    depth, g, n = lam_re.shape
    c = b_re.shape[-1]
    small = pl.BlockSpec((None, g, n), lambda l: (l, 0, 0))
    wide = pl.BlockSpec((None, g, n * c), lambda l: (l, 0, 0))
    return pl.pallas_call(
        _s5prep_kernel,
        grid=(depth,),
        in_specs=[small, small, pl.BlockSpec((None, g, 1), lambda l: (l, 0, 0)), wide, wide, wide, wide],
        out_specs=[small, small, wide, wide],
        out_shape=[jax.ShapeDtypeStruct((depth, g, n), F32)] * 2 + [jax.ShapeDtypeStruct((depth, g, n * c), F32)] * 2,
        compiler_params=_cparams(("arbitrary",), 32),
        name="s5prep",
    )(lam_re, lam_im, log_dt.reshape(depth, g, 1),
      jnp.repeat(lam_re, c, axis=-1), jnp.repeat(lam_im, c, axis=-1),
      b_re.reshape(depth, g, n * c), b_im.reshape(depth, g, n * c))


def _s5_kernel(x_ref, h0_ref, bset_ref, cset_ref, ar_ref, ai_ref, dsk_ref, wglu_ref, bglu_ref,
               y_ref, hfin_ref, u_ref, hst_ref):
    rows = x_ref.shape[0]
    nb = SUBLANES
    half = SET_H // 2

    @pl.when(pl.program_id(0) == 0)
    def _():
        hst_ref[...] = h0_ref[...]

    x = x_ref[...]
    xb = x.astype(BF16)
    for s in range(S5_SETS):
        u_ref[:, s * SET_H:(s + 1) * SET_H] = _dot(xb[:, s * SET_X:(s + 1) * SET_X], bset_ref[s])

    ar, ai = ar_ref[...], ai_ref[...]

    def step(t, h):
        r0 = pl.multiple_of(t * nb, nb)
        u = u_ref[pl.ds(r0, nb), :]
        out = []
        for s in range(S5_SETS):
            o = s * SET_H
            hr, hi = h[:, o:o + half], h[:, o + half:o + SET_H]
            a_r, a_i = ar[:, s * half:(s + 1) * half], ai[:, s * half:(s + 1) * half]
            out.append(a_r * hr - a_i * hi + u[:, o:o + half])
            out.append(a_r * hi + a_i * hr + u[:, o + half:o + SET_H])
        hn = jnp.concatenate(out, axis=1)
        u_ref[pl.ds(r0, nb), :] = hn
        return hn

    h = lax.fori_loop(0, rows // nb, step, hst_ref[...])
    hst_ref[...] = h
    hfin_ref[...] = h

    hb = u_ref[...].astype(BF16)
    y = jnp.concatenate([_dot(hb[:, s * SET_H:(s + 1) * SET_H], cset_ref[s]) for s in range(S5_SETS)], axis=1)
    y = y + dsk_ref[...] * x
    y = 0.5 * y * (1.0 + jnp.tanh(0.7978845608028654 * (y + 0.044715 * (y * y * y))))
    y_ref[...] = y * jax.nn.sigmoid(_dot(y.astype(BF16), wglu_ref[...]) + bglu_ref[...])


def _s5(x_tb, h0, bset, cset, ar, ai, d_skip, w_glu, b_glu, tt):
    rows = x_tb.shape[0]
    blk = tt * SUBLANES
    const2 = lambda shape: pl.BlockSpec(shape, lambda i: (0, 0))
    const3 = lambda shape: pl.BlockSpec(shape, lambda i: (0, 0, 0))
    return pl.pallas_call(
        _s5_kernel,
        grid=(rows // blk,),
        in_specs=[pl.BlockSpec((blk, D_SSM), lambda i: (i, 0)),
                  const2((SUBLANES, D_STATE)),
                  const3((S5_SETS, SET_X, SET_H)), const3((S5_SETS, SET_H, SET_X)),
                  const2((SUBLANES, D_STATE // 2)), const2((SUBLANES, D_STATE // 2)),
                  const2((1, D_SSM)), const2((D_SSM, D_SSM)), const2((1, D_SSM))],
        out_specs=[pl.BlockSpec((blk, D_SSM), lambda i: (i, 0)), const2((SUBLANES, D_STATE))],
        out_shape=[jax.ShapeDtypeStruct((rows, D_SSM), F32), jax.ShapeDtypeStruct((SUBLANES, D_STATE), F32)],
        scratch_shapes=[pltpu.VMEM((blk, D_STATE), F32), pltpu.VMEM((SUBLANES, D_STATE), F32)],
        compiler_params=_cparams(("arbitrary",), 48),
        name="s5",
    )(x_tb, h0, bset, cset, ar, ai, d_skip, w_glu, b_glu)


def _s5_operands(abr, abi, bbr, bbi, c_re, c_im):
    g, n, c = N_SSM_GROUPS, SSM_STATE, SSM_GROUP
    gs = g // S5_SETS
    eye = jnp.eye(gs, dtype=F32)

    def in_map(bb):
        bb = bb.reshape(S5_SETS, gs, n, c).transpose(0, 1, 3, 2)
        return (bb[:, :, :, None, :] * eye[None, :, None, :, None]).reshape(S5_SETS, gs * c, gs * n)

    def out_map(cc):
        cc = cc.reshape(S5_SETS, gs, c, n).transpose(0, 1, 3, 2)
        return (cc[:, :, :, None, :] * eye[None, :, None, :, None]).reshape(S5_SETS, gs * n, gs * c)

    bset = jnp.concatenate([in_map(bbr), in_map(bbi)], axis=2).astype(BF16)
    cset = jnp.concatenate([out_map(c_re), -out_map(c_im)], axis=1).astype(BF16)
    ar = jnp.broadcast_to(abr.reshape(1, g * n), (SUBLANES, g * n))
    ai = jnp.broadcast_to(abi.reshape(1, g * n), (SUBLANES, g * n))
    return bset, cset, ar, ai


def _state_to_lanes(h_re, h_im):
    nb = h_re.shape[0]
    gs = N_SSM_GROUPS // S5_SETS
    st = jnp.stack([h_re.reshape(nb, S5_SETS, gs, SSM_STATE), h_im.reshape(nb, S5_SETS, gs, SSM_STATE)], axis=2)
    return st.reshape(nb, D_STATE)


def _lanes_to_state(h):
    nb = h.shape[0]
    gs = N_SSM_GROUPS // S5_SETS
    st = h.reshape(nb, S5_SETS, 2, gs, SSM_STATE)
    return (st[:, :, 0].reshape(nb, N_SSM_GROUPS, SSM_STATE), st[:, :, 1].reshape(nb, N_SSM_GROUPS, SSM_STATE))


def _cmlp_kernel(lc, emit_vn, u_ref, v_ref, gm_ref, g_ref, b_ref, w_ref, bias_ref, o_ref, *vn_ref):
    tm = u_ref.shape[0]
    v = v_ref[...]
    mu = jnp.mean(v, axis=-1, keepdims=True)
    d = v - mu
    var = jnp.mean(d * d, axis=-1, keepdims=True)
    vn = d * lax.rsqrt(var + EPS) * g_ref[...] + b_ref[...]
    if emit_vn:
        vn_ref[0][...] = vn
    row = lax.broadcasted_iota(jnp.int32, (lc, lc), 0)
    col = lax.broadcasted_iota(jnp.int32, (lc, lc), 1)
    ug = u_ref[...] * _silu(gm_ref[...])
    vb = vn.astype(BF16)
    for g in range(N_CMLP_GROUPS):
        wg = jnp.where(col <= row, w_ref[g], 0.0).astype(BF16)
        lanes = slice(g * CMLP_GROUP, (g + 1) * CMLP_GROUP)
        for c in range(tm // lc):
            rws = slice(c * lc, (c + 1) * lc)
            zc = _dot(wg, vb[rws, lanes]) + bias_ref[:, lanes]
            o_ref[rws, lanes] = (ug[rws, lanes] * zc).astype(BF16)


def _cmlp(z, ln_g, ln_b, w, bias, tm, lc, emit_vn):
    t = z.shape[0]
    col = lambda c: pl.BlockSpec((tm, D_CMLP), lambda i: (i, c))
    const2 = lambda shape: pl.BlockSpec(shape, lambda i: (0, 0))
    out_specs = [pl.BlockSpec((tm, D_CMLP), lambda i: (i, 0))]
    out_shape = [jax.ShapeDtypeStruct((t, D_CMLP), BF16)]
    if emit_vn:
        out_specs.append(pl.BlockSpec((tm, D_CMLP), lambda i: (i, 0)))
        out_shape.append(jax.ShapeDtypeStruct((t, D_CMLP), F32))
    return pl.pallas_call(
        functools.partial(_cmlp_kernel, lc, emit_vn),
        grid=(t // tm,),
        in_specs=[col(COL_U // D_CMLP), col(COL_VV // D_CMLP), col(COL_GM // D_CMLP),
                  const2((1, D_CMLP)), const2((1, D_CMLP)),
                  pl.BlockSpec((N_CMLP_GROUPS, lc, lc), lambda i: (0, 0, 0)),
                  const2((lc, D_CMLP))],
        out_specs=out_specs,
        out_shape=out_shape,
        compiler_params=_cparams(("arbitrary",), 32),
        name="cmlp",
    )(z, z, z, ln_g, ln_b, w, bias)


def _outproj_kernel(x_ref, a_ref, s_ref, gs_ref, m_ref, gate_ref, w_ref, o_ref):
    sg = (s_ref[...] * _silu(gs_ref[...])).astype(BF16)
    acc = (_dot(a_ref[...].astype(BF16), w_ref[0:D_ATTN, :])
           + _dot(sg, w_ref[D_ATTN:D_ATTN + D_SSM, :])
           + _dot(m_ref[...], w_ref[D_ATTN + D_SSM:, :]))
    o_ref[...] = x_ref[...] + gate_ref[...] * acc


def _outproj(x, a, s, z, m, gate, w_out, tm, rows_per_mod):
    t = x.shape[0]
    mod_rows = gate.shape[1]
    return pl.pallas_call(
        _outproj_kernel,
        grid=(t // tm,),
        in_specs=[pl.BlockSpec((tm, D_MODEL), lambda i: (i, 0)),
                  pl.BlockSpec((tm, D_ATTN), lambda i: (i, 0)),
                  pl.BlockSpec((tm, D_SSM), lambda i: (i, 0)),
                  pl.BlockSpec((tm, D_SSM), lambda i: (i, COL_GS // D_SSM)),
                  pl.BlockSpec((tm, D_CMLP), lambda i: (i, 0)),
                  pl.BlockSpec((None, mod_rows, D_MODEL), lambda i: ((i * tm) // rows_per_mod, 0, 0)),
                  pl.BlockSpec((D_MODEL, D_MODEL), lambda i: (0, 0))],
        out_specs=pl.BlockSpec((tm, D_MODEL), lambda i: (i, 0)),
        out_shape=jax.ShapeDtypeStruct((t, D_MODEL), F32),
        compiler_params=_cparams(("arbitrary",), 56),
        name="outproj",
    )(x, a, s, z, m, gate, w_out)


def _pack_w_in(w_in):
    d_q = 3 * D_ATTN
    wt = w_in.transpose(0, 2, 1)
    main = jnp.concatenate([wt[:, :d_q], wt[:, d_q + N_HEADS:]], axis=1).astype(BF16)
    wt_fg = jnp.pad(wt[:, d_q:d_q + N_HEADS], ((0, 0), (0, LANES - N_HEADS), (0, 0))).astype(BF16)
    return main, wt_fg


def _layer(x, mods, attn_fn, h0_lanes, seq_len, batch, wl, consts, tm, tt, lc, prompt):
    t = x.shape[0]
    shift, scale, gate, rows_per_mod = mods
    z, logf = _inproj(x, wl["norm_g"], scale, shift, wl["wt_main"], wl["wt_fg"], wl["b_f"], tm, rows_per_mod)
    qk = _qkprep(z, logf, wl["qg"], wl["kg"], consts, min(tm, TQ), seq_len, prompt)
    a_out = attn_fn(qk, z)

    xs = z[:, COL_XS:COL_XS + D_SSM].reshape(batch, seq_len, D_SSM).transpose(1, 0, 2)
    xs = jnp.pad(xs, ((0, 0), (0, SUBLANES - batch), (0, 0))).reshape(seq_len * SUBLANES, D_SSM)
    s_tb, hfin = _s5(xs, h0_lanes, wl["bset"], wl["cset"], wl["ar"], wl["ai"], wl["d_skip"],
                     wl["w_glu"], wl["b_glu"], tt)
    s_pre = s_tb.reshape(seq_len, SUBLANES, D_SSM)[:, :batch].transpose(1, 0, 2).reshape(t, D_SSM)

    m_res = _cmlp(z, wl["ln_g"], wl["ln_b"], wl["w_s"], wl["b_s"], min(tm, 512), lc, not prompt)
    y = _outproj(x, a_out, s_pre, z, m_res[0], gate, wl["w_out"], min(tm, 512), rows_per_mod)
    h_re, h_im = _lanes_to_state(hfin[:batch])
    if prompt:
        _, _, knt, vt32, _, lft, _ = qk
        return y, (knt, vt32, lft, h_re, h_im)
    _, _, kn, _, _ = qk
    return y, (kn, z[:, COL_V:COL_V + D_ATTN], logf[:, :N_HEADS], h_re, h_im, m_res[1])


def kernel(x_prompt, x_sample, c_prompt, c_sample, cache_k, cache_v, cache_logf, state_ssm_re, state_ssm_im,
           page_table, norm_g, w_ada, b_ada, w_in, b_f, q_norm_g, k_norm_g, lam_re, lam_im, log_dt,
           b_re, b_im, c_re, c_im, d_skip, w_glu, b_glu, sgu_ln_g, sgu_ln_b, w_s, b_s, w_out):
    depth = w_in.shape[0]
    bp, seq, _ = x_prompt.shape
    bd, t_dec, _ = x_sample.shape
    n_pool = cache_k.shape[0]

    c_all = jnp.concatenate([c_prompt, c_sample], axis=0)
    c_all = jnp.pad(c_all, ((0, 2 * SUBLANES - bp - bd), (0, 0)))
    mod = _ada_all(c_all, w_ada, b_ada)

    wt_main, wt_fg = _pack_w_in(w_in)
    w_out_b = w_out.astype(BF16)
    w_glu_b = w_glu.astype(BF16)
    b_f_pad = jnp.pad(b_f, ((0, 0), (0, LANES - N_HEADS))).reshape(depth, 1, LANES)
    consts = _aug_placement()
    abr, abi, bbr, bbi = _s5prep_all(lam_re, lam_im, log_dt, b_re, b_im)

    kt_pages = cache_k.transpose(0, 1, 3, 4, 2).reshape(n_pool, depth, D_ATTN, PAGE_SIZE)
    vt_pages = cache_v.transpose(0, 1, 3, 4, 2).reshape(n_pool, depth, D_ATTN, PAGE_SIZE)
    srev = _srev_all(cache_logf.transpose(0, 1, 3, 2), page_table)

    eye_d = jnp.eye(bd, dtype=F32)
    w_s_dec = (eye_d[None, None, :, None, :, None] * w_s[:, :, None, :t_dec, None, :t_dec]).reshape(
        depth, N_CMLP_GROUPS, bd * t_dec, bd * t_dec)
    bias_p = jnp.repeat(b_s.transpose(0, 2, 1), CMLP_GROUP, axis=-1)
    bias_d = jnp.tile(bias_p[:, :t_dec], (1, bd, 1))

    yp = x_prompt.reshape(bp * seq, D_MODEL)
    ys = x_sample.reshape(bd * t_dec, D_MODEL)
    h_zero = jnp.zeros((SUBLANES, D_STATE), F32)
    eye_h = jnp.eye(N_HEADS, dtype=BF16)
    outs_p, outs_s = [], []
    for l in range(depth):
        bset, cset, ar, ai = _s5_operands(abr[l], abi[l], bbr[l], bbi[l], c_re[l], c_im[l])
        wl = dict(norm_g=norm_g[l][None], wt_main=wt_main[l], wt_fg=wt_fg[l], b_f=b_f_pad[l],
                  qg=jnp.tile(q_norm_g[l], N_HEADS)[None], kg=jnp.tile(k_norm_g[l], N_HEADS)[None],
                  bset=bset, cset=cset, ar=ar, ai=ai, d_skip=d_skip[l][None], w_glu=w_glu_b[l],
                  b_glu=b_glu[l][None], ln_g=sgu_ln_g[l][None], ln_b=sgu_ln_b[l][None], w_out=w_out_b[l])
        shift, scale, gate = (mod[l][:, i * D_MODEL:(i + 1) * D_MODEL] for i in range(3))

        mods_p = (shift[:bp, None], scale[:bp, None], gate[:bp, None], seq)
        attn_p = lambda qk, z: _fox_prompt(qk[0], qk[1], qk[4], z, bp, seq, TQ)
        yp, cache_p = _layer(yp, mods_p, attn_p, h_zero, seq, bp, dict(wl, w_s=w_s[l], b_s=bias_p[l]),
                             consts, 1024, 64, CHUNK, True)
        outs_p.append(cache_p)

        rep = lambda m: jnp.repeat(m[bp:bp + bd], t_dec, axis=0)[None]
        mods_s = (rep(shift), rep(scale), rep(gate), bd * t_dec)

        def attn_s(qk, z, l=l):
            qa, ka, _, vb, f = qk
            q = qa.reshape(bd, t_dec, N_HEADS, 2 * HEAD_DIM)[..., :HEAD_DIM]
            qbd = (q[:, :, :, None, :] * eye_h[None, None, :, :, None]).reshape(bd, t_dec * N_HEADS, D_ATTN)
            kb = ka.reshape(bd, t_dec, N_HEADS, 2 * HEAD_DIM)[..., :HEAD_DIM].reshape(bd, t_dec, D_ATTN)
            knew = jnp.pad(kb, ((0, 0), (0, LANES - t_dec), (0, 0)))
            vnew = jnp.pad(vb.reshape(bd, t_dec, D_ATTN), ((0, 0), (0, LANES - t_dec), (0, 0)))
            cn = f[:, :N_HEADS].reshape(bd, t_dec, N_HEADS)
            cn_col = jnp.broadcast_to(cn.reshape(bd, t_dec * N_HEADS, 1), (bd, t_dec * N_HEADS, LANES))
            cn_row = jnp.broadcast_to(cn.transpose(0, 2, 1)[:, None], (bd, t_dec, N_HEADS, t_dec))
            cn_row = jnp.pad(cn_row.reshape(bd, t_dec * N_HEADS, t_dec), ((0, 0), (0, 0), (0, LANES - t_dec)))
            return _fox_sample(l, page_table, qbd, kt_pages, vt_pages, srev, knew, vnew, cn_col, cn_row, z)

        h0 = _state_to_lanes(state_ssm_re[:, l], state_ssm_im[:, l])
        ys, cache_s = _layer(ys, mods_s, attn_s, h0, t_dec, bd, dict(wl, w_s=w_s_dec[l], b_s=bias_d[l]),
                             consts, bd * t_dec, t_dec, bd * t_dec, False)
        outs_s.append(cache_s)

    def stack(outs, idx, shape):
        return jnp.stack([o[idx].reshape(shape) for o in outs], axis=1)

    k_prompt = stack(outs_p, 0, (bp, N_HEADS, HEAD_DIM, seq)).transpose(0, 1, 4, 2, 3)
    v_prompt = stack(outs_p, 1, (bp, N_HEADS, HEAD_DIM, seq)).transpose(0, 1, 4, 2, 3)
    logf_prompt = stack(outs_p, 2, (bp, N_HEADS, seq)).transpose(0, 1, 3, 2)
    return (yp.reshape(bp, seq, D_MODEL), ys.reshape(bd, t_dec, D_MODEL),
            k_prompt, v_prompt, logf_prompt,
            stack(outs_p, 3, (bp, N_SSM_GROUPS, SSM_STATE)), stack(outs_p, 4, (bp, N_SSM_GROUPS, SSM_STATE)),
            stack(outs_s, 0, (bd, t_dec, N_HEADS, HEAD_DIM)), stack(outs_s, 1, (bd, t_dec, N_HEADS, HEAD_DIM)),
            stack(outs_s, 2, (bd, t_dec, N_HEADS)),
            stack(outs_s, 3, (bd, N_SSM_GROUPS, SSM_STATE)), stack(outs_s, 4, (bd, N_SSM_GROUPS, SSM_STATE)),
            stack(outs_s, 5, (bd, t_dec, D_CMLP)))
```

```python
import functools

import jax
import jax.numpy as jnp
from jax import lax
from jax.experimental import pallas as pl
from jax.experimental.pallas import tpu as pltpu

F32 = jnp.float32
BF16 = jnp.bfloat16

D_MODEL = 2048
HEAD_DIM = 64
D_ATTN = 1024
N_HEADS = 16
D_SSM = 512
SSM_GROUP = 16
N_SSM_GROUPS = 32
SSM_STATE = 64
D_CMLP = 512
CHUNK = 128
N_CMLP_GROUPS = 4
CMLP_GROUP = 128
PAGE_SIZE = 128
EPS = 1e-6
LANES = 128
SUBLANES = 8
NEG_BIG = -1e30
LOG2E = 1.4426950408889634

COL_Q, COL_K, COL_V, COL_GA = 0, 1024, 2048, 3072
COL_XS, COL_GS, COL_U, COL_VV, COL_GM = 4096, 4608, 5120, 5632, 6144
D_Z = 6656
D_STATE = 2 * N_SSM_GROUPS * SSM_STATE
S5_SETS = 2
SET_X = D_SSM // S5_SETS
SET_H = D_STATE // S5_SETS
PAGES_PER_STEP = 8
TQ = 512
INPROJ_TN = 1664


def _cparams(sem, vmem_mb):
    return pltpu.CompilerParams(dimension_semantics=sem, vmem_limit_bytes=vmem_mb * 1024 * 1024)


def _split3(x):
    hi = x.astype(BF16)
    r1 = x - hi.astype(F32)
    mid = r1.astype(BF16)
    lo = (r1 - mid.astype(F32)).astype(BF16)
    return hi, mid, lo


def _dot(a, b):
    return jnp.dot(a, b, preferred_element_type=F32)


def _dot_nt(a, b):
    return lax.dot_general(a, b, (((1,), (1,)), ((), ())), preferred_element_type=F32)


def _silu(x):
    return x * jax.nn.sigmoid(x)


def _ada_kernel(c_ref, w_ref, b_ref, o_ref):
    s = _silu(c_ref[...]).astype(BF16)
    o_ref[...] = _dot(s, w_ref[...].astype(BF16)) + b_ref[...]


def _ada_all(c_all, w_ada, b_ada):
    depth = w_ada.shape[0]
    rows = c_all.shape[0]
    tn = 1024
    return pl.pallas_call(
        _ada_kernel,
        grid=(depth, 3 * D_MODEL // tn),
        in_specs=[
            pl.BlockSpec((rows, D_MODEL), lambda l, j: (0, 0)),
            pl.BlockSpec((None, D_MODEL, tn), lambda l, j: (l, 0, j)),
            pl.BlockSpec((None, 1, tn), lambda l, j: (l, 0, j)),
        ],
        out_specs=pl.BlockSpec((None, rows, tn), lambda l, j: (l, 0, j)),
        out_shape=jax.ShapeDtypeStruct((depth, rows, 3 * D_MODEL), F32),
        compiler_params=_cparams(("arbitrary", "arbitrary"), 48),
        name="ada",
    )(c_all, w_ada, b_ada.reshape(depth, 1, 3 * D_MODEL))


def _inproj_kernel(x_ref, g_ref, sc_ref, sh_ref, wt_ref, wfgt_ref, bf_ref, z_ref, lf_ref, h_ref):
    @pl.when(pl.program_id(1) == 0)
    def _():
        x = x_ref[...]
        ms = jnp.mean(x * x, axis=-1, keepdims=True)
        y = x * lax.rsqrt(ms + EPS) * g_ref[...]
        hb = (y * (1.0 + sc_ref[...]) + sh_ref[...]).astype(BF16)
        h_ref[...] = hb
        fg = _dot_nt(hb, wfgt_ref[...]) + bf_ref[...]
        lf_ref[...] = jnp.minimum(fg, 0.0) - jnp.log1p(jnp.exp(-jnp.abs(fg)))

    z_ref[...] = _dot_nt(h_ref[...], wt_ref[...])


def _inproj(x, norm_g, scale, shift, wt_main, wt_fg, b_f, tm, rows_per_mod):
    t = x.shape[0]
    tn = INPROJ_TN
    mod_rows = scale.shape[1]
    mod_spec = pl.BlockSpec((None, mod_rows, D_MODEL), lambda i, j: ((i * tm) // rows_per_mod, 0, 0))
    return pl.pallas_call(
        _inproj_kernel,
        grid=(t // tm, D_Z // tn),
        in_specs=[
            pl.BlockSpec((tm, D_MODEL), lambda i, j: (i, 0)),
            pl.BlockSpec((1, D_MODEL), lambda i, j: (0, 0)),
            mod_spec,
            mod_spec,
            pl.BlockSpec((tn, D_MODEL), lambda i, j: (j, 0)),
            pl.BlockSpec((LANES, D_MODEL), lambda i, j: (0, 0)),
            pl.BlockSpec((1, LANES), lambda i, j: (0, 0)),
        ],
        out_specs=[
            pl.BlockSpec((tm, tn), lambda i, j: (i, j)),
            pl.BlockSpec((tm, LANES), lambda i, j: (i, 0)),
        ],
        out_shape=[jax.ShapeDtypeStruct((t, D_Z), F32), jax.ShapeDtypeStruct((t, LANES), F32)],
        scratch_shapes=[pltpu.VMEM((tm, D_MODEL), BF16)],
        compiler_params=_cparams(("arbitrary", "arbitrary"), 60),
        name="inproj",
    )(x, norm_g, scale, shift, wt_main, wt_fg, b_f)


def _qkprep_kernel(seq_len, prompt, q_ref, k_ref, v_ref, lf_ref, qg_ref, kg_ref, seg_ref, pq_ref, pk_ref,
                   qa_ref, ka_ref, *rest):
    if prompt:
        knt_ref, vt32_ref, vt16_ref, lft_ref, f_ref, carry_ref = rest
    else:
        kn_ref, vb_ref, f_ref, carry_ref = rest
    tm = q_ref.shape[0]
    pairs = N_HEADS // 2
    row = lax.broadcasted_iota(jnp.int32, (tm, tm), 0)
    col = lax.broadcasted_iota(jnp.int32, (tm, tm), 1)
    tri = col <= row
    if seq_len < tm:
        shift = seq_len.bit_length() - 1
        tri = tri & (lax.shift_right_logical(row, shift) == lax.shift_right_logical(col, shift))
    tri_b = jnp.where(tri, 1.0, 0.0).astype(BF16)
    lf = lf_ref[...]
    hi, mid, lo = _split3(lf)
    f = _dot(tri_b, hi) + _dot(tri_b, mid) + _dot(tri_b, lo)
    if seq_len > tm:
        @pl.when(pl.program_id(0) % (seq_len // tm) == 0)
        def _():
            carry_ref[...] = jnp.zeros_like(carry_ref)

        f = f + carry_ref[...]
        carry_ref[...] = f[tm - 1:tm, :]
    f_ref[...] = f

    def head_norm(x, g):
        ss = _dot((x * x).astype(BF16), seg_ref[...])
        return x * lax.rsqrt(ss * (1.0 / HEAD_DIM) + EPS) * g

    qs = head_norm(q_ref[...], qg_ref[...]) * (HEAD_DIM ** -0.5 * LOG2E)
    kn = head_norm(k_ref[...], kg_ref[...])
    v = v_ref[...]
    if prompt:
        lft_ref[...] = lf.T[0:N_HEADS, :]
        for p in range(pairs):
            cols = slice(p * LANES, (p + 1) * LANES)
            knt_ref[cols, :] = kn[:, cols].T
            vt = v[:, cols].T
            vt32_ref[cols, :] = vt
            vt16_ref[p] = vt.astype(BF16)
    else:
        kn_ref[...] = kn
        vb_ref[...] = v.astype(BF16)

    fh, fm, fl = _split3(f * LOG2E)
    lane = lax.broadcasted_iota(jnp.int32, (tm, LANES), 1)
    parts = jnp.where(lane < 16, fh.astype(F32),
                      jnp.where(lane < 32, pltpu.roll(fm.astype(F32), 16, 1),
                                jnp.where(lane < 48, pltpu.roll(fl.astype(F32), 32, 1),
                                          jnp.where(lane == 48, 1.0, 0.0)))).astype(BF16)
    ext_q = _dot(parts, pq_ref[...])
    ext_k = _dot(parts, pk_ref[...])
    low = lane < HEAD_DIM
    for src, ext, dst in ((qs, ext_q, qa_ref), (kn, ext_k, ka_ref)):
        for p in range(pairs):
            blk = src[:, p * LANES:(p + 1) * LANES]
            e = ext[:, p * LANES:(p + 1) * LANES]
            dst[:, (2 * p) * LANES:(2 * p + 1) * LANES] = jnp.where(low, blk, pltpu.roll(e, 64, 1)).astype(BF16)
            dst[:, (2 * p + 1) * LANES:(2 * p + 2) * LANES] = jnp.where(low, pltpu.roll(blk, 64, 1), e).astype(BF16)


def _aug_placement():
    r = jnp.arange(LANES)[:, None]
    c = jnp.arange(N_HEADS * HEAD_DIM)[None, :]
    rp, rh = r // 16, r % 16
    ch, cc = c // HEAD_DIM, c % HEAD_DIM
    f_rows = (r < 48) & (rh == ch)
    one_row = r == 48
    pq = jnp.where(f_rows & (cc == rp), 1.0, 0.0) + jnp.where(one_row & (cc >= 3) & (cc < 6), 1.0, 0.0)
    pk = jnp.where(f_rows & (cc == rp + 3), -1.0, 0.0) + jnp.where(one_row & (cc < 3), 1.0, 0.0)
    seg = jnp.where(jnp.arange(D_ATTN)[:, None] // HEAD_DIM == jnp.arange(D_ATTN)[None, :] // HEAD_DIM, 1.0, 0.0)
    return pq.astype(BF16), pk.astype(BF16), seg.astype(BF16)


def _qkprep(z, logf, qg, kg, consts, tm, seq_len, prompt):
    t = z.shape[0]
    pq, pk, seg = consts
    col = lambda c: pl.BlockSpec((tm, D_ATTN), lambda i: (i, c))
    const = lambda shape: pl.BlockSpec(shape, lambda i: (0, 0))
    rows = lambda width: pl.BlockSpec((tm, width), lambda i: (i, 0))
    pairs = N_HEADS // 2
    out_specs = [rows(2 * D_ATTN), rows(2 * D_ATTN)]
    out_shape = [jax.ShapeDtypeStruct((t, 2 * D_ATTN), BF16), jax.ShapeDtypeStruct((t, 2 * D_ATTN), BF16)]
    if prompt:
        nk = seq_len // tm
        batch = t // seq_len
        tspec = lambda width: pl.BlockSpec((None, width, tm), lambda i: (i // nk, 0, i % nk))
        out_specs += [tspec(D_ATTN), tspec(D_ATTN),
                      pl.BlockSpec((None, pairs, None, LANES, tm), lambda i: (i // nk, 0, i % nk, 0, 0)),
                      tspec(N_HEADS)]
        out_shape += [jax.ShapeDtypeStruct((batch, D_ATTN, seq_len), F32),
                      jax.ShapeDtypeStruct((batch, D_ATTN, seq_len), F32),
                      jax.ShapeDtypeStruct((batch, pairs, nk, LANES, tm), BF16),
                      jax.ShapeDtypeStruct((batch, N_HEADS, seq_len), F32)]
    else:
        out_specs += [rows(D_ATTN), rows(D_ATTN)]
        out_shape += [jax.ShapeDtypeStruct((t, D_ATTN), F32), jax.ShapeDtypeStruct((t, D_ATTN), BF16)]
    out_specs.append(rows(LANES))
    out_shape.append(jax.ShapeDtypeStruct((t, LANES), F32))
    return pl.pallas_call(
        functools.partial(_qkprep_kernel, seq_len, prompt),
        grid=(t // tm,),
        in_specs=[col(COL_Q // D_ATTN), col(COL_K // D_ATTN), col(COL_V // D_ATTN), rows(LANES),
                  const((1, D_ATTN)), const((1, D_ATTN)), const((D_ATTN, D_ATTN)),
                  const((LANES, D_ATTN)), const((LANES, D_ATTN))],
        out_specs=out_specs,
        out_shape=out_shape,
        scratch_shapes=[pltpu.VMEM((1, LANES), F32)],
        compiler_params=_cparams(("arbitrary",), 56),
        name="qkprep",
    )(z, z, z, logf, qg, kg, seg, pq, pk)


def _fox_step(i, qa_ref, ka_ref, vt_ref, ga_ref, o_ref, m_ref, l_ref, acc_ref):
    tq = qa_ref.shape[0]
    tk = vt_ref.shape[2]
    m_ref[...] = jnp.full_like(m_ref, NEG_BIG)
    l_ref[...] = jnp.zeros_like(l_ref)
    acc_ref[...] = jnp.zeros_like(acc_ref)

    def tile(j, masked):
        k0 = pl.multiple_of(j * tk, tk)
        for hh in range(2):
            k = ka_ref[pl.ds(k0, tk), hh * LANES:(hh + 1) * LANES]
            s = _dot_nt(k, qa_ref[:, hh * LANES:(hh + 1) * LANES])
            if masked:
                row = lax.broadcasted_iota(jnp.int32, (tk, tq), 0)
                col = lax.broadcasted_iota(jnp.int32, (tk, tq), 1)
                s = jnp.where(row <= col, s, NEG_BIG)
            m_prev = m_ref[hh]
            m_new = jnp.maximum(m_prev, jnp.max(s, axis=0, keepdims=True))
            alpha = jnp.exp2(m_prev - m_new)
            p = jnp.exp2(s - m_new)
            l_ref[hh] = alpha * l_ref[hh] + jnp.sum(p, axis=0, keepdims=True)
            vt = vt_ref[j, hh * HEAD_DIM:(hh + 1) * HEAD_DIM, :]
            acc_ref[hh] = alpha * acc_ref[hh] + _dot(vt, p.astype(BF16))
            m_ref[hh] = m_new

    def body(j, carry):
        tile(j, False)
        return carry

    lax.fori_loop(0, i, body, 0)
    tile(i, True)
    o_t = jnp.concatenate([acc_ref[0] / l_ref[0], acc_ref[1] / l_ref[1]], axis=0)
    o_ref[...] = (o_t.T * _silu(ga_ref[...])).astype(BF16)


def _foxdec_step(first, last, q_ref, kt_refs, vt_refs, lf_refs, knew_ref, vnew_ref, cncol_ref, cnrow_ref, ga_ref,
                 o_ref, m_ref, l_ref, acc_ref, carry_ref):
    n = len(kt_refs)
    rows = q_ref.shape[0]
    npos = n * PAGE_SIZE

    @pl.when(first)
    def _():
        m_ref[...] = jnp.full_like(m_ref, NEG_BIG)
        l_ref[...] = jnp.zeros_like(l_ref)
        acc_ref[...] = jnp.zeros_like(acc_ref)
        carry_ref[...] = jnp.zeros_like(carry_ref)

    def accumulate(s, pv):
        m_prev = m_ref[...]
        m_new = jnp.maximum(m_prev, jnp.max(s, axis=1, keepdims=True))
        alpha = jnp.exp2(m_prev - m_new)
        p = jnp.exp2(s - m_new)
        l_ref[...] = alpha * l_ref[...] + jnp.sum(p, axis=1, keepdims=True)
        acc_ref[...] = alpha * acc_ref[...] + pv(p.astype(BF16))
        m_ref[...] = m_new

    row = lax.broadcasted_iota(jnp.int32, (PAGE_SIZE, PAGE_SIZE), 0)
    col = lax.broadcasted_iota(jnp.int32, (PAGE_SIZE, PAGE_SIZE), 1)
    later = jnp.where(row > col, 1.0, 0.0).astype(BF16)
    srev = [None] * n
    for i in reversed(range(n)):
        x = lf_refs[i][...]
        hi, mid, lo = _split3(x)
        srev[i] = _dot(hi, later) + _dot(mid, later) + _dot(lo, later) + carry_ref[...]
        carry_ref[...] = carry_ref[...] + jnp.sum(x, axis=1, keepdims=True)
    srev = jnp.concatenate(srev, axis=1) * LOG2E

    q = q_ref[...]
    cn_col = cncol_ref[...] * LOG2E
    kt = jnp.concatenate([r[...].astype(BF16) for r in kt_refs], axis=1)
    vt = jnp.concatenate([r[...].astype(BF16) for r in vt_refs], axis=1)
    s = _dot(q, kt)
    s = (s.reshape(rows // N_HEADS, N_HEADS, npos) + srev[None]).reshape(rows, npos)
    s = s + jnp.concatenate([cn_col] * n, axis=1)
    accumulate(s, lambda p: _dot_nt(p, vt))

    @pl.when(last)
    def _():
        sn = _dot_nt(q, knew_ref[...]) + (cn_col - cnrow_ref[...] * LOG2E)
        row = lax.broadcasted_iota(jnp.int32, (rows, LANES), 0)
        col = lax.broadcasted_iota(jnp.int32, (rows, LANES), 1)
        sn = jnp.where(col <= lax.shift_right_logical(row, 4), sn, NEG_BIG)
        accumulate(sn, lambda p: _dot(p, vnew_ref[...]))
        o = acc_ref[...] / l_ref[...]
        row = lax.broadcasted_iota(jnp.int32, (rows, D_ATTN), 0)
        col = lax.broadcasted_iota(jnp.int32, (rows, D_ATTN), 1)
        own = (row & (N_HEADS - 1)) == lax.shift_right_logical(col, 6)
        o = jnp.sum(jnp.where(own, o, 0.0).reshape(rows // N_HEADS, N_HEADS, D_ATTN), axis=1)
        o_ref[...] = o * _silu(ga_ref[...])


def _attn_kernel(nq, nchunk, pt_ref, *refs):
    n = PAGES_PER_STEP
    qa_ref, ka_ref, vtp_ref, gap_ref, q_ref = refs[:5]
    kt_refs = refs[5:5 + n]
    vt_refs = refs[5 + n:5 + 2 * n]
    lf_refs = refs[5 + 2 * n:5 + 3 * n]
    (knew_ref, vnew_ref, cncol_ref, cnrow_ref, gas_ref, op_ref, os_ref,
     mp_ref, lp_ref, accp_ref, ms_ref, ls_ref, accs_ref, carry_ref) = refs[5 + 3 * n:]
    g = pl.program_id(0)
    _fox_step(g % nq, qa_ref, ka_ref, vtp_ref, gap_ref, op_ref, mp_ref, lp_ref, accp_ref)
    c = g % nchunk
    _foxdec_step(c == 0, c == nchunk - 1, q_ref, kt_refs, vt_refs, lf_refs, knew_ref, vnew_ref, cncol_ref,
                 cnrow_ref, gas_ref, os_ref, ms_ref, ls_ref, accs_ref, carry_ref)


def _attention(layer, page_table, qa, ka, vt, z_p, batch, seq, qbd, kt_pages, vt_pages, lf_pages, knew, vnew,
               cn_col, cn_row, z_s):
    tq = vt.shape[4]
    nq = seq // tq
    pairs = N_HEADS // 2
    n_dec, n_pages = page_table.shape
    n = PAGES_PER_STEP
    nchunk = n_pages // n
    steps = batch * pairs * nq
    assert steps == n_dec * nchunk, (steps, n_dec, nchunk)
    rows = qbd.shape[1]
    t_len = rows // N_HEADS
    ga_blk = COL_GA // LANES

    def prow(g):
        return (g // (pairs * nq)) * nq + g % nq

    pcol = lambda g: (g // nq) % pairs
    drow = lambda g: g // nchunk
    dchunk = lambda g: nchunk - 1 - g % nchunk

    def page_spec(shape, i):
        return pl.BlockSpec((None, None) + shape, lambda g, pt: (pt[drow(g), dchunk(g) * n + i], layer, 0, 0))

    per_row = lambda shape: pl.BlockSpec((None,) + shape, lambda g, pt: (drow(g), 0, 0))
    grid_spec = pltpu.PrefetchScalarGridSpec(
        num_scalar_prefetch=1,
        grid=(steps,),
        in_specs=([pl.BlockSpec((tq, 2 * LANES), lambda g, pt: (prow(g), pcol(g))),
                   pl.BlockSpec((seq, 2 * LANES), lambda g, pt: (g // (pairs * nq), pcol(g))),
                   pl.BlockSpec((None, None) + vt.shape[2:], lambda g, pt: (g // (pairs * nq), pcol(g), 0, 0, 0)),
                   pl.BlockSpec((tq, LANES), lambda g, pt: (prow(g), ga_blk + pcol(g))),
                   per_row((rows, D_ATTN))]
                  + [page_spec((D_ATTN, PAGE_SIZE), i) for i in range(n)]
                  + [page_spec((D_ATTN, PAGE_SIZE), i) for i in range(n)]
                  + [page_spec((N_HEADS, PAGE_SIZE), i) for i in range(n)]
                  + [per_row((LANES, D_ATTN)), per_row((LANES, D_ATTN)),
                     per_row((rows, LANES)), per_row((rows, LANES)),
                     pl.BlockSpec((t_len, D_ATTN), lambda g, pt: (drow(g), COL_GA // D_ATTN))]),
        out_specs=[pl.BlockSpec((tq, LANES), lambda g, pt: (prow(g), pcol(g))),
                   pl.BlockSpec((t_len, D_ATTN), lambda g, pt: (drow(g), 0))],
        scratch_shapes=[pltpu.VMEM((2, 1, tq), F32), pltpu.VMEM((2, 1, tq), F32), pltpu.VMEM((2, HEAD_DIM, tq), F32),
                        pltpu.VMEM((rows, 1), F32), pltpu.VMEM((rows, 1), F32), pltpu.VMEM((rows, D_ATTN), F32),
                        pltpu.VMEM((N_HEADS, LANES), F32)],
    )
    return pl.pallas_call(
        functools.partial(_attn_kernel, nq, nchunk),
        grid_spec=grid_spec,
        out_shape=[jax.ShapeDtypeStruct((batch * seq, D_ATTN), BF16),
                   jax.ShapeDtypeStruct((n_dec * t_len, D_ATTN), F32)],
        compiler_params=_cparams(("arbitrary",), 56),
        name="attn",
    )(page_table, qa, ka, vt, z_p, qbd, *([kt_pages] * n), *([vt_pages] * n), *([lf_pages] * n),
      knew, vnew, cn_col, cn_row, z_s)


def _s5prep_kernel(lr_ref, li_ref, ldt_ref, lrx_ref, lix_ref, bre_ref, bim_ref,
                   abr_ref, abi_ref, bbr_ref, bbi_ref):
    dt = jnp.exp(ldt_ref[...])

    def discretise(lr, li):
        mag = jnp.exp(lr * dt)
        abr = mag * jnp.cos(li * dt)
        abi = mag * jnp.sin(li * dt)
        den = lr * lr + li * li
        pr = abr - 1.0
        return abr, abi, (pr * lr + abi * li) / den, (abi * lr - pr * li) / den

    abr, abi, _, _ = discretise(lr_ref[...], li_ref[...])
    abr_ref[...] = abr
    abi_ref[...] = abi
    _, _, fr, fi = discretise(lrx_ref[...], lix_ref[...])
    bre, bim = bre_ref[...], bim_ref[...]
    bbr_ref[...] = fr * bre - fi * bim
    bbi_ref[...] = fr * bim + fi * bre


def _s5prep_all(lam_re, lam_im, log_dt, b_re, b_im):
    depth, g, n = lam_re.shape
    c = b_re.shape[-1]
    small = pl.BlockSpec((None, g, n), lambda l: (l, 0, 0))
    wide = pl.BlockSpec((None, g, n * c), lambda l: (l, 0, 0))
    return pl.pallas_call(
        _s5prep_kernel,
        grid=(depth,),
        in_specs=[small, small, pl.BlockSpec((None, g, 1), lambda l: (l, 0, 0)), wide, wide, wide, wide],
        out_specs=[small, small, wide, wide],
        out_shape=[jax.ShapeDtypeStruct((depth, g, n), F32)] * 2 + [jax.ShapeDtypeStruct((depth, g, n * c), F32)] * 2,
        compiler_params=_cparams(("arbitrary",), 32),
        name="s5prep",
    )(lam_re, lam_im, log_dt.reshape(depth, g, 1),
      jnp.repeat(lam_re, c, axis=-1), jnp.repeat(lam_im, c, axis=-1),
      b_re.reshape(depth, g, n * c), b_im.reshape(depth, g, n * c))


def _s5_kernel(x_ref, h0_ref, bset_ref, cset_ref, ar_ref, ai_ref, dsk_ref, wglu_ref, bglu_ref,
               y_ref, hfin_ref, u_ref, hst_ref):
    rows = x_ref.shape[0]
    nb = SUBLANES
    half = SET_H // 2

    @pl.when(pl.program_id(0) == 0)
    def _():
        hst_ref[...] = h0_ref[...]

    x = x_ref[...]
    xb = x.astype(BF16)
    for s in range(S5_SETS):
        u_ref[:, s * SET_H:(s + 1) * SET_H] = _dot(xb[:, s * SET_X:(s + 1) * SET_X], bset_ref[s])

    ar, ai = ar_ref[...], ai_ref[...]

    def step(t, h):
        r0 = pl.multiple_of(t * nb, nb)
        u = u_ref[pl.ds(r0, nb), :]
        out = []
        for s in range(S5_SETS):
            o = s * SET_H
            hr, hi = h[:, o:o + half], h[:, o + half:o + SET_H]
            a_r, a_i = ar[:, s * half:(s + 1) * half], ai[:, s * half:(s + 1) * half]
            out.append(a_r * hr - a_i * hi + u[:, o:o + half])
            out.append(a_r * hi + a_i * hr + u[:, o + half:o + SET_H])
        hn = jnp.concatenate(out, axis=1)
        u_ref[pl.ds(r0, nb), :] = hn
        return hn

    h = lax.fori_loop(0, rows // nb, step, hst_ref[...])
    hst_ref[...] = h
    hfin_ref[...] = h

    hb = u_ref[...].astype(BF16)
    y = jnp.concatenate([_dot(hb[:, s * SET_H:(s + 1) * SET_H], cset_ref[s]) for s in range(S5_SETS)], axis=1)
    y = y + dsk_ref[...] * x
    y = 0.5 * y * (1.0 + jnp.tanh(0.7978845608028654 * (y + 0.044715 * (y * y * y))))
    y_ref[...] = y * jax.nn.sigmoid(_dot(y.astype(BF16), wglu_ref[...]) + bglu_ref[...])


def _s5(x_tb, h0, bset, cset, ar, ai, d_skip, w_glu, b_glu, tt):
    rows = x_tb.shape[0]
    blk = tt * SUBLANES
    const2 = lambda shape: pl.BlockSpec(shape, lambda i: (0, 0))
    const3 = lambda shape: pl.BlockSpec(shape, lambda i: (0, 0, 0))
    return pl.pallas_call(
        _s5_kernel,
        grid=(rows // blk,),
        in_specs=[pl.BlockSpec((blk, D_SSM), lambda i: (i, 0)),
                  const2((SUBLANES, D_STATE)),
                  const3((S5_SETS, SET_X, SET_H)), const3((S5_SETS, SET_H, SET_X)),
                  const2((SUBLANES, D_STATE // 2)), const2((SUBLANES, D_STATE // 2)),
                  const2((1, D_SSM)), const2((D_SSM, D_SSM)), const2((1, D_SSM))],
        out_specs=[pl.BlockSpec((blk, D_SSM), lambda i: (i, 0)), const2((SUBLANES, D_STATE))],
        out_shape=[jax.ShapeDtypeStruct((rows, D_SSM), F32), jax.ShapeDtypeStruct((SUBLANES, D_STATE), F32)],
        scratch_shapes=[pltpu.VMEM((blk, D_STATE), F32), pltpu.VMEM((SUBLANES, D_STATE), F32)],
        compiler_params=_cparams(("arbitrary",), 48),
        name="s5",
    )(x_tb, h0, bset, cset, ar, ai, d_skip, w_glu, b_glu)


def _s5_operands(abr, abi, bbr, bbi, c_re, c_im):
    g, n, c = N_SSM_GROUPS, SSM_STATE, SSM_GROUP
    gs = g // S5_SETS
    eye = jnp.eye(gs, dtype=F32)

    def in_map(bb):
        bb = bb.reshape(S5_SETS, gs, n, c).transpose(0, 1, 3, 2)
        return (bb[:, :, :, None, :] * eye[None, :, None, :, None]).reshape(S5_SETS, gs * c, gs * n)

    def out_map(cc):
        cc = cc.reshape(S5_SETS, gs, c, n).transpose(0, 1, 3, 2)
        return (cc[:, :, :, None, :] * eye[None, :, None, :, None]).reshape(S5_SETS, gs * n, gs * c)

    bset = jnp.concatenate([in_map(bbr), in_map(bbi)], axis=2).astype(BF16)
    cset = jnp.concatenate([out_map(c_re), -out_map(c_im)], axis=1).astype(BF16)
    ar = jnp.broadcast_to(abr.reshape(1, g * n), (SUBLANES, g * n))
    ai = jnp.broadcast_to(abi.reshape(1, g * n), (SUBLANES, g * n))
    return bset, cset, ar, ai


def _state_to_lanes(h_re, h_im):
    nb = h_re.shape[0]
    gs = N_SSM_GROUPS // S5_SETS
    st = jnp.stack([h_re.reshape(nb, S5_SETS, gs, SSM_STATE), h_im.reshape(nb, S5_SETS, gs, SSM_STATE)], axis=2)
    return st.reshape(nb, D_STATE)


def _lanes_to_state(h):
    nb = h.shape[0]
    gs = N_SSM_GROUPS // S5_SETS
    st = h.reshape(nb, S5_SETS, 2, gs, SSM_STATE)
    return (st[:, :, 0].reshape(nb, N_SSM_GROUPS, SSM_STATE), st[:, :, 1].reshape(nb, N_SSM_GROUPS, SSM_STATE))


def _cmlp_kernel(lc, emit_vn, u_ref, v_ref, gm_ref, g_ref, b_ref, w_ref, bias_ref, o_ref, *vn_ref):
    tm = u_ref.shape[0]
    v = v_ref[...]
    mu = jnp.mean(v, axis=-1, keepdims=True)
    d = v - mu
    var = jnp.mean(d * d, axis=-1, keepdims=True)
    vn = d * lax.rsqrt(var + EPS) * g_ref[...] + b_ref[...]
    if emit_vn:
        vn_ref[0][...] = vn
    row = lax.broadcasted_iota(jnp.int32, (lc, lc), 0)
    col = lax.broadcasted_iota(jnp.int32, (lc, lc), 1)
    ug = u_ref[...] * _silu(gm_ref[...])
    vb = vn.astype(BF16)
    for g in range(N_CMLP_GROUPS):
        wg = jnp.where(col <= row, w_ref[g], 0.0).astype(BF16)
        lanes = slice(g * CMLP_GROUP, (g + 1) * CMLP_GROUP)
        for c in range(tm // lc):
            rws = slice(c * lc, (c + 1) * lc)
            zc = _dot(wg, vb[rws, lanes]) + bias_ref[:, lanes]
            o_ref[rws, lanes] = (ug[rws, lanes] * zc).astype(BF16)


def _cmlp(z, ln_g, ln_b, w, bias, tm, lc, emit_vn):
    t = z.shape[0]
    col = lambda c: pl.BlockSpec((tm, D_CMLP), lambda i: (i, c))
    const2 = lambda shape: pl.BlockSpec(shape, lambda i: (0, 0))
    out_specs = [pl.BlockSpec((tm, D_CMLP), lambda i: (i, 0))]
    out_shape = [jax.ShapeDtypeStruct((t, D_CMLP), BF16)]
    if emit_vn:
        out_specs.append(pl.BlockSpec((tm, D_CMLP), lambda i: (i, 0)))
        out_shape.append(jax.ShapeDtypeStruct((t, D_CMLP), F32))
    return pl.pallas_call(
        functools.partial(_cmlp_kernel, lc, emit_vn),
        grid=(t // tm,),
        in_specs=[col(COL_U // D_CMLP), col(COL_VV // D_CMLP), col(COL_GM // D_CMLP),
                  const2((1, D_CMLP)), const2((1, D_CMLP)),
                  pl.BlockSpec((N_CMLP_GROUPS, lc, lc), lambda i: (0, 0, 0)),
                  const2((lc, D_CMLP))],
        out_specs=out_specs,
        out_shape=out_shape,
        compiler_params=_cparams(("arbitrary",), 32),
        name="cmlp",
    )(z, z, z, ln_g, ln_b, w, bias)


def _outproj_kernel(x_ref, a_ref, s_ref, gs_ref, m_ref, gate_ref, w_ref, o_ref):
    sg = (s_ref[...] * _silu(gs_ref[...])).astype(BF16)
    acc = (_dot(a_ref[...].astype(BF16), w_ref[0:D_ATTN, :])
           + _dot(sg, w_ref[D_ATTN:D_ATTN + D_SSM, :])
           + _dot(m_ref[...], w_ref[D_ATTN + D_SSM:, :]))
    o_ref[...] = x_ref[...] + gate_ref[...] * acc


def _outproj(x, a, s, z, m, gate, w_out, tm, rows_per_mod):
    t = x.shape[0]
    mod_rows = gate.shape[1]
    return pl.pallas_call(
        _outproj_kernel,
        grid=(t // tm,),
        in_specs=[pl.BlockSpec((tm, D_MODEL), lambda i: (i, 0)),
                  pl.BlockSpec((tm, D_ATTN), lambda i: (i, 0)),
                  pl.BlockSpec((tm, D_SSM), lambda i: (i, 0)),
                  pl.BlockSpec((tm, D_SSM), lambda i: (i, COL_GS // D_SSM)),
                  pl.BlockSpec((tm, D_CMLP), lambda i: (i, 0)),
                  pl.BlockSpec((None, mod_rows, D_MODEL), lambda i: ((i * tm) // rows_per_mod, 0, 0)),
                  pl.BlockSpec((D_MODEL, D_MODEL), lambda i: (0, 0))],
        out_specs=pl.BlockSpec((tm, D_MODEL), lambda i: (i, 0)),
        out_shape=jax.ShapeDtypeStruct((t, D_MODEL), F32),
        compiler_params=_cparams(("arbitrary",), 56),
        name="outproj",
    )(x, a, s, z, m, gate, w_out)


def _pack_w_in(w_in):
    d_q = 3 * D_ATTN
    wt = w_in.transpose(0, 2, 1)
    main = jnp.concatenate([wt[:, :d_q], wt[:, d_q + N_HEADS:]], axis=1).astype(BF16)
    wt_fg = jnp.pad(wt[:, d_q:d_q + N_HEADS], ((0, 0), (0, LANES - N_HEADS), (0, 0))).astype(BF16)
    return main, wt_fg


def _layer_front(x, mods, wl, consts, tm, seq_len, prompt):
    shift, scale, _, rows_per_mod = mods
    z, logf = _inproj(x, wl["norm_g"], scale, shift, wl["wt_main"], wl["wt_fg"], wl["b_f"], tm, rows_per_mod)
    return z, logf, _qkprep(z, logf, wl["qg"], wl["kg"], consts, min(tm, TQ), seq_len, prompt)


def _layer_back(x, z, logf, qk, a_out, mods, h0_lanes, seq_len, batch, wl, tm, tt, lc, prompt):
    t = x.shape[0]
    gate, rows_per_mod = mods[2], mods[3]
    xs = z[:, COL_XS:COL_XS + D_SSM].reshape(batch, seq_len, D_SSM).transpose(1, 0, 2)
    xs = jnp.pad(xs, ((0, 0), (0, SUBLANES - batch), (0, 0))).reshape(seq_len * SUBLANES, D_SSM)
    s_tb, hfin = _s5(xs, h0_lanes, wl["bset"], wl["cset"], wl["ar"], wl["ai"], wl["d_skip"],
                     wl["w_glu"], wl["b_glu"], tt)
    s_pre = s_tb.reshape(seq_len, SUBLANES, D_SSM)[:, :batch].transpose(1, 0, 2).reshape(t, D_SSM)

    m_res = _cmlp(z, wl["ln_g"], wl["ln_b"], wl["w_s"], wl["b_s"], min(tm, 512), lc, not prompt)
    y = _outproj(x, a_out, s_pre, z, m_res[0], gate, wl["w_out"], min(tm, 512), rows_per_mod)
    h_re, h_im = _lanes_to_state(hfin[:batch])
    if prompt:
        _, _, knt, vt32, _, lft, _ = qk
        return y, (knt, vt32, lft, h_re, h_im)
    _, _, kn, _, _ = qk
    return y, (kn, z[:, COL_V:COL_V + D_ATTN], logf[:, :N_HEADS], h_re, h_im, m_res[1])


def kernel(x_prompt, x_sample, c_prompt, c_sample, cache_k, cache_v, cache_logf, state_ssm_re, state_ssm_im,
           page_table, norm_g, w_ada, b_ada, w_in, b_f, q_norm_g, k_norm_g, lam_re, lam_im, log_dt,
           b_re, b_im, c_re, c_im, d_skip, w_glu, b_glu, sgu_ln_g, sgu_ln_b, w_s, b_s, w_out):
    depth = w_in.shape[0]
    bp, seq, _ = x_prompt.shape
    bd, t_dec, _ = x_sample.shape
    n_pool = cache_k.shape[0]

    c_all = jnp.concatenate([c_prompt, c_sample], axis=0)
    c_all = jnp.pad(c_all, ((0, 2 * SUBLANES - bp - bd), (0, 0)))
    mod = _ada_all(c_all, w_ada, b_ada)

    wt_main, wt_fg = _pack_w_in(w_in)
    w_out_b = w_out.astype(BF16)
    w_glu_b = w_glu.astype(BF16)
    b_f_pad = jnp.pad(b_f, ((0, 0), (0, LANES - N_HEADS))).reshape(depth, 1, LANES)
    consts = _aug_placement()
    abr, abi, bbr, bbi = _s5prep_all(lam_re, lam_im, log_dt, b_re, b_im)

    kt_pages = cache_k.transpose(0, 1, 3, 4, 2).reshape(n_pool, depth, D_ATTN, PAGE_SIZE)
    vt_pages = cache_v.transpose(0, 1, 3, 4, 2).reshape(n_pool, depth, D_ATTN, PAGE_SIZE)
    lf_pages = cache_logf.transpose(0, 1, 3, 2)

    eye_d = jnp.eye(bd, dtype=F32)
    w_s_dec = (eye_d[None, None, :, None, :, None] * w_s[:, :, None, :t_dec, None, :t_dec]).reshape(
        depth, N_CMLP_GROUPS, bd * t_dec, bd * t_dec)
    bias_p = jnp.repeat(b_s.transpose(0, 2, 1), CMLP_GROUP, axis=-1)
    bias_d = jnp.tile(bias_p[:, :t_dec], (1, bd, 1))

    yp = x_prompt.reshape(bp * seq, D_MODEL)
    ys = x_sample.reshape(bd * t_dec, D_MODEL)
    h_zero = jnp.zeros((SUBLANES, D_STATE), F32)
    eye_h = jnp.eye(N_HEADS, dtype=BF16)
    outs_p, outs_s = [], []
    for l in range(depth):
        bset, cset, ar, ai = _s5_operands(abr[l], abi[l], bbr[l], bbi[l], c_re[l], c_im[l])
        wl = dict(norm_g=norm_g[l][None], wt_main=wt_main[l], wt_fg=wt_fg[l], b_f=b_f_pad[l],
                  qg=jnp.tile(q_norm_g[l], N_HEADS)[None], kg=jnp.tile(k_norm_g[l], N_HEADS)[None],
                  bset=bset, cset=cset, ar=ar, ai=ai, d_skip=d_skip[l][None], w_glu=w_glu_b[l],
                  b_glu=b_glu[l][None], ln_g=sgu_ln_g[l][None], ln_b=sgu_ln_b[l][None], w_out=w_out_b[l])
        wl_p = dict(wl, w_s=w_s[l], b_s=bias_p[l])
        wl_s = dict(wl, w_s=w_s_dec[l], b_s=bias_d[l])
        shift, scale, gate = (mod[l][:, i * D_MODEL:(i + 1) * D_MODEL] for i in range(3))
        mods_p = (shift[:bp, None], scale[:bp, None], gate[:bp, None], seq)
        rep = lambda m: jnp.repeat(m[bp:bp + bd], t_dec, axis=0)[None]
        mods_s = (rep(shift), rep(scale), rep(gate), bd * t_dec)

        z_p, logf_p, qk_p = _layer_front(yp, mods_p, wl_p, consts, 1024, seq, True)
        z_s, logf_s, qk_s = _layer_front(ys, mods_s, wl_s, consts, bd * t_dec, t_dec, False)

        qa, ka, _, vb, f = qk_s
        q = qa.reshape(bd, t_dec, N_HEADS, 2 * HEAD_DIM)[..., :HEAD_DIM]
        qbd = (q[:, :, :, None, :] * eye_h[None, None, :, :, None]).reshape(bd, t_dec * N_HEADS, D_ATTN)
        kb = ka.reshape(bd, t_dec, N_HEADS, 2 * HEAD_DIM)[..., :HEAD_DIM].reshape(bd, t_dec, D_ATTN)
        knew = jnp.pad(kb, ((0, 0), (0, LANES - t_dec), (0, 0)))
        vnew = jnp.pad(vb.reshape(bd, t_dec, D_ATTN), ((0, 0), (0, LANES - t_dec), (0, 0)))
        cn = f[:, :N_HEADS].reshape(bd, t_dec, N_HEADS)
        cn_col = jnp.broadcast_to(cn.reshape(bd, t_dec * N_HEADS, 1), (bd, t_dec * N_HEADS, LANES))
        cn_row = jnp.broadcast_to(cn.transpose(0, 2, 1)[:, None], (bd, t_dec, N_HEADS, t_dec))
        cn_row = jnp.pad(cn_row.reshape(bd, t_dec * N_HEADS, t_dec), ((0, 0), (0, 0), (0, LANES - t_dec)))
        a_p, a_s = _attention(l, page_table, qk_p[0], qk_p[1], qk_p[4], z_p, bp, seq, qbd, kt_pages, vt_pages,
                              lf_pages, knew, vnew, cn_col, cn_row, z_s)

        yp, cache_p = _layer_back(yp, z_p, logf_p, qk_p, a_p, mods_p, h_zero, seq, bp, wl_p, 1024, 64, CHUNK, True)
        h0 = _state_to_lanes(state_ssm_re[:, l], state_ssm_im[:, l])
        ys, cache_s = _layer_back(ys, z_s, logf_s, qk_s, a_s, mods_s, h0, t_dec, bd, wl_s, bd * t_dec, t_dec,
                                  bd * t_dec, False)
        outs_p.append(cache_p)
        outs_s.append(cache_s)

    def stack(outs, idx, shape):
        return jnp.stack([o[idx].reshape(shape) for o in outs], axis=1)

    k_prompt = stack(outs_p, 0, (bp, N_HEADS, HEAD_DIM, seq)).transpose(0, 1, 4, 2, 3)
    v_prompt = stack(outs_p, 1, (bp, N_HEADS, HEAD_DIM, seq)).transpose(0, 1, 4, 2, 3)
    logf_prompt = stack(outs_p, 2, (bp, N_HEADS, seq)).transpose(0, 1, 3, 2)
    return (yp.reshape(bp, seq, D_MODEL), ys.reshape(bd, t_dec, D_MODEL),
            k_prompt, v_prompt, logf_prompt,
            stack(outs_p, 3, (bp, N_SSM_GROUPS, SSM_STATE)), stack(outs_p, 4, (bp, N_SSM_GROUPS, SSM_STATE)),
            stack(outs_s, 0, (bd, t_dec, N_HEADS, HEAD_DIM)), stack(outs_s, 1, (bd, t_dec, N_HEADS, HEAD_DIM)),
            stack(outs_s, 2, (bd, t_dec, N_HEADS)),
            stack(outs_s, 3, (bd, N_SSM_GROUPS, SSM_STATE)), stack(outs_s, 4, (bd, N_SSM_GROUPS, SSM_STATE)),
            stack(outs_s, 5, (bd, t_dec, D_CMLP)))
```

```python
import functools

import jax
import jax.numpy as jnp
from jax import lax
from jax.experimental import pallas as pl
from jax.experimental.pallas import tpu as pltpu

F32 = jnp.float32
BF16 = jnp.bfloat16

D_MODEL = 2048
HEAD_DIM = 64
D_ATTN = 1024
N_HEADS = 16
D_SSM = 512
SSM_GROUP = 16
N_SSM_GROUPS = 32
SSM_STATE = 64
D_CMLP = 512
CHUNK = 128
N_CMLP_GROUPS = 4
CMLP_GROUP = 128
PAGE_SIZE = 128
EPS = 1e-6
LANES = 128
SUBLANES = 8
NEG_BIG = -1e30
LOG2E = 1.4426950408889634

COL_Q, COL_K, COL_V, COL_GA = 0, 1024, 2048, 3072
COL_XS, COL_GS, COL_U, COL_VV, COL_GM = 4096, 4608, 5120, 5632, 6144
D_Z = 6656
D_STATE = 2 * N_SSM_GROUPS * SSM_STATE
S5_SETS = 2
SET_X = D_SSM // S5_SETS
SET_H = D_STATE // S5_SETS
PAGES_PER_STEP = 8
TQ = 512
INPROJ_TN = 1664


def _cparams(sem, vmem_mb):
    return pltpu.CompilerParams(dimension_semantics=sem, vmem_limit_bytes=vmem_mb * 1024 * 1024)


def _split3(x):
    hi = x.astype(BF16)
    r1 = x - hi.astype(F32)
    mid = r1.astype(BF16)
    lo = (r1 - mid.astype(F32)).astype(BF16)
    return hi, mid, lo


def _dot(a, b):
    return jnp.dot(a, b, preferred_element_type=F32)


def _dot_nt(a, b):
    return lax.dot_general(a, b, (((1,), (1,)), ((), ())), preferred_element_type=F32)


def _silu(x):
    return x * jax.nn.sigmoid(x)


def _ada_kernel(c_ref, w_ref, b_ref, o_ref):
    s = _silu(c_ref[...]).astype(BF16)
    o_ref[...] = _dot(s, w_ref[...].astype(BF16)) + b_ref[...]


def _ada_all(c_all, w_ada, b_ada):
    depth = w_ada.shape[0]
    rows = c_all.shape[0]
    tn = 1024
    return pl.pallas_call(
        _ada_kernel,
        grid=(depth, 3 * D_MODEL // tn),
        in_specs=[
            pl.BlockSpec((rows, D_MODEL), lambda l, j: (0, 0)),
            pl.BlockSpec((None, D_MODEL, tn), lambda l, j: (l, 0, j)),
            pl.BlockSpec((None, 1, tn), lambda l, j: (l, 0, j)),
        ],
        out_specs=pl.BlockSpec((None, rows, tn), lambda l, j: (l, 0, j)),
        out_shape=jax.ShapeDtypeStruct((depth, rows, 3 * D_MODEL), F32),
        compiler_params=_cparams(("arbitrary", "arbitrary"), 48),
        name="ada",
    )(c_all, w_ada, b_ada.reshape(depth, 1, 3 * D_MODEL))


def _inproj_kernel(x_ref, g_ref, sc_ref, sh_ref, wt_ref, wfgt_ref, bf_ref, z_ref, lf_ref, h_ref):
    @pl.when(pl.program_id(1) == 0)
    def _():
        x = x_ref[...]
        ms = jnp.mean(x * x, axis=-1, keepdims=True)
        y = x * lax.rsqrt(ms + EPS) * g_ref[...]
        hb = (y * (1.0 + sc_ref[...]) + sh_ref[...]).astype(BF16)
        h_ref[...] = hb
        fg = _dot_nt(hb, wfgt_ref[...]) + bf_ref[...]
        lf_ref[...] = jnp.minimum(fg, 0.0) - jnp.log1p(jnp.exp(-jnp.abs(fg)))

    z_ref[...] = _dot_nt(h_ref[...], wt_ref[...])


def _inproj(layer, x, norm_g, scale, shift, wt_main, wt_fg, b_f, tm, rows_per_mod):
    t = x.shape[0]
    tn = INPROJ_TN
    mod_rows = scale.shape[1]
    mod_spec = pl.BlockSpec((None, mod_rows, D_MODEL), lambda i, j: ((i * tm) // rows_per_mod, 0, 0))
    return pl.pallas_call(
        _inproj_kernel,
        grid=(t // tm, D_Z // tn),
        in_specs=[
            pl.BlockSpec((tm, D_MODEL), lambda i, j: (i, 0)),
            pl.BlockSpec((1, D_MODEL), lambda i, j: (0, 0)),
            mod_spec,
            mod_spec,
            pl.BlockSpec((None, tn, D_MODEL), lambda i, j: (layer, j, 0)),
            pl.BlockSpec((None, LANES, D_MODEL), lambda i, j: (layer, 0, 0)),
            pl.BlockSpec((1, LANES), lambda i, j: (0, 0)),
        ],
        out_specs=[
            pl.BlockSpec((tm, tn), lambda i, j: (i, j)),
            pl.BlockSpec((tm, LANES), lambda i, j: (i, 0)),
        ],
        out_shape=[jax.ShapeDtypeStruct((t, D_Z), F32), jax.ShapeDtypeStruct((t, LANES), F32)],
        scratch_shapes=[pltpu.VMEM((tm, D_MODEL), BF16)],
        compiler_params=_cparams(("arbitrary", "arbitrary"), 60),
        name="inproj",
    )(x, norm_g, scale, shift, wt_main, wt_fg, b_f)


def _qkprep_kernel(seq_len, prompt, n_alias, q_ref, k_ref, v_ref, lf_ref, qg_ref, kg_ref, seg_ref, pq_ref, pk_ref,
                   *rest):
    qa_ref, ka_ref = rest[n_alias:n_alias + 2]
    rest = rest[n_alias + 2:]
    if prompt:
        knt_ref, vt32_ref, vt16_ref, lft_ref, f_ref, carry_ref = rest
    else:
        kn_ref, vb_ref, f_ref, carry_ref = rest
    tm = q_ref.shape[0]
    pairs = N_HEADS // 2
    row = lax.broadcasted_iota(jnp.int32, (tm, tm), 0)
    col = lax.broadcasted_iota(jnp.int32, (tm, tm), 1)
    tri = col <= row
    if seq_len < tm:
        shift = seq_len.bit_length() - 1
        tri = tri & (lax.shift_right_logical(row, shift) == lax.shift_right_logical(col, shift))
    tri_b = jnp.where(tri, 1.0, 0.0).astype(BF16)
    lf = lf_ref[...]
    hi, mid, lo = _split3(lf)
    f = _dot(tri_b, hi) + _dot(tri_b, mid) + _dot(tri_b, lo)
    if seq_len > tm:
        @pl.when(pl.program_id(0) % (seq_len // tm) == 0)
        def _():
            carry_ref[...] = jnp.zeros_like(carry_ref)

        f = f + carry_ref[...]
        carry_ref[...] = f[tm - 1:tm, :]
    f_ref[...] = f

    def head_norm(x, g):
        ss = _dot((x * x).astype(BF16), seg_ref[...])
        return x * lax.rsqrt(ss * (1.0 / HEAD_DIM) + EPS) * g

    qs = head_norm(q_ref[...], qg_ref[...]) * (HEAD_DIM ** -0.5 * LOG2E)
    kn = head_norm(k_ref[...], kg_ref[...])
    v = v_ref[...]
    if prompt:
        lft_ref[...] = lf.T[0:N_HEADS, :]
        for p in range(pairs):
            cols = slice(p * LANES, (p + 1) * LANES)
            knt_ref[cols, :] = kn[:, cols].T
            vt = v[:, cols].T
            vt32_ref[cols, :] = vt
            vt16_ref[p] = vt.astype(BF16)
    else:
        kn_ref[...] = kn
        vb_ref[...] = v.astype(BF16)

    fh, fm, fl = _split3(f * LOG2E)
    lane = lax.broadcasted_iota(jnp.int32, (tm, LANES), 1)
    parts = jnp.where(lane < 16, fh.astype(F32),
                      jnp.where(lane < 32, pltpu.roll(fm.astype(F32), 16, 1),
                                jnp.where(lane < 48, pltpu.roll(fl.astype(F32), 32, 1),
                                          jnp.where(lane == 48, 1.0, 0.0)))).astype(BF16)
    ext_q = _dot(parts, pq_ref[...])
    ext_k = _dot(parts, pk_ref[...])
    low = lane < HEAD_DIM
    for src, ext, dst in ((qs, ext_q, qa_ref), (kn, ext_k, ka_ref)):
        for p in range(pairs):
            blk = src[:, p * LANES:(p + 1) * LANES]
            e = ext[:, p * LANES:(p + 1) * LANES]
            dst[:, (2 * p) * LANES:(2 * p + 1) * LANES] = jnp.where(low, blk, pltpu.roll(e, 64, 1)).astype(BF16)
            dst[:, (2 * p + 1) * LANES:(2 * p + 2) * LANES] = jnp.where(low, pltpu.roll(blk, 64, 1), e).astype(BF16)


def _aug_placement():
    r = jnp.arange(LANES)[:, None]
    c = jnp.arange(N_HEADS * HEAD_DIM)[None, :]
    rp, rh = r // 16, r % 16
    ch, cc = c // HEAD_DIM, c % HEAD_DIM
    f_rows = (r < 48) & (rh == ch)
    one_row = r == 48
    pq = jnp.where(f_rows & (cc == rp), 1.0, 0.0) + jnp.where(one_row & (cc >= 3) & (cc < 6), 1.0, 0.0)
    pk = jnp.where(f_rows & (cc == rp + 3), -1.0, 0.0) + jnp.where(one_row & (cc < 3), 1.0, 0.0)
    seg = jnp.where(jnp.arange(D_ATTN)[:, None] // HEAD_DIM == jnp.arange(D_ATTN)[None, :] // HEAD_DIM, 1.0, 0.0)
    return pq.astype(BF16), pk.astype(BF16), seg.astype(BF16)


def _qkprep(z, logf, qg, kg, consts, tm, seq_len, prompt, layer=0, depth=1, caches=None):
    t = z.shape[0]
    pq, pk, seg = consts
    col = lambda c: pl.BlockSpec((tm, D_ATTN), lambda i: (i, c))
    const = lambda shape: pl.BlockSpec(shape, lambda i: (0, 0))
    rows = lambda width: pl.BlockSpec((tm, width), lambda i: (i, 0))
    pairs = N_HEADS // 2
    out_specs = [rows(2 * D_ATTN), rows(2 * D_ATTN)]
    out_shape = [jax.ShapeDtypeStruct((t, 2 * D_ATTN), BF16), jax.ShapeDtypeStruct((t, 2 * D_ATTN), BF16)]
    if prompt:
        nk = seq_len // tm
        batch = t // seq_len
        tspec = lambda width: pl.BlockSpec((None, None, width, tm), lambda i: (i // nk, layer, 0, i % nk))
        out_specs += [tspec(D_ATTN), tspec(D_ATTN),
                      pl.BlockSpec((None, pairs, None, LANES, tm), lambda i: (i // nk, 0, i % nk, 0, 0)),
                      tspec(N_HEADS)]
        out_shape += [jax.ShapeDtypeStruct((batch, depth, D_ATTN, seq_len), F32),
                      jax.ShapeDtypeStruct((batch, depth, D_ATTN, seq_len), F32),
                      jax.ShapeDtypeStruct((batch, pairs, nk, LANES, tm), BF16),
                      jax.ShapeDtypeStruct((batch, depth, N_HEADS, seq_len), F32)]
    else:
        out_specs += [rows(D_ATTN), rows(D_ATTN)]
        out_shape += [jax.ShapeDtypeStruct((t, D_ATTN), F32), jax.ShapeDtypeStruct((t, D_ATTN), BF16)]
    out_specs.append(rows(LANES))
    out_shape.append(jax.ShapeDtypeStruct((t, LANES), F32))
    in_specs = [col(COL_Q // D_ATTN), col(COL_K // D_ATTN), col(COL_V // D_ATTN), rows(LANES),
                const((1, D_ATTN)), const((1, D_ATTN)), const((D_ATTN, D_ATTN)),
                const((LANES, D_ATTN)), const((LANES, D_ATTN))]
    caches = tuple(caches or ())
    aliases = {}
    if caches:
        aliases = {len(in_specs): 2, len(in_specs) + 1: 3, len(in_specs) + 2: 5}
        in_specs = in_specs + [pl.BlockSpec(memory_space=pl.ANY)] * len(caches)
    return pl.pallas_call(
        functools.partial(_qkprep_kernel, seq_len, prompt, len(caches)),
        grid=(t // tm,),
        in_specs=in_specs,
        out_specs=out_specs,
        out_shape=out_shape,
        scratch_shapes=[pltpu.VMEM((1, LANES), F32)],
        input_output_aliases=aliases,
        compiler_params=_cparams(("arbitrary",), 56),
        name="qkprep",
    )(z, z, z, logf, qg, kg, seg, pq, pk, *caches)


def _fox_step(i, qa_ref, ka_ref, vt_ref, ga_ref, o_ref, m_ref, l_ref, acc_ref):
    tq = qa_ref.shape[0]
    tk = vt_ref.shape[2]
    m_ref[...] = jnp.full_like(m_ref, NEG_BIG)
    l_ref[...] = jnp.zeros_like(l_ref)
    acc_ref[...] = jnp.zeros_like(acc_ref)

    def tile(j, masked):
        k0 = pl.multiple_of(j * tk, tk)
        for hh in range(2):
            k = ka_ref[pl.ds(k0, tk), hh * LANES:(hh + 1) * LANES]
            s = _dot_nt(k, qa_ref[:, hh * LANES:(hh + 1) * LANES])
            if masked:
                row = lax.broadcasted_iota(jnp.int32, (tk, tq), 0)
                col = lax.broadcasted_iota(jnp.int32, (tk, tq), 1)
                s = jnp.where(row <= col, s, NEG_BIG)
            m_prev = m_ref[hh]
            m_new = jnp.maximum(m_prev, jnp.max(s, axis=0, keepdims=True))
            alpha = jnp.exp2(m_prev - m_new)
            p = jnp.exp2(s - m_new)
            l_ref[hh] = alpha * l_ref[hh] + jnp.sum(p, axis=0, keepdims=True)
            vt = vt_ref[j, hh * HEAD_DIM:(hh + 1) * HEAD_DIM, :]
            acc_ref[hh] = alpha * acc_ref[hh] + _dot(vt, p.astype(BF16))
            m_ref[hh] = m_new

    def body(j, carry):
        tile(j, False)
        return carry

    lax.fori_loop(0, i, body, 0)
    tile(i, True)
    o_t = jnp.concatenate([acc_ref[0] / l_ref[0], acc_ref[1] / l_ref[1]], axis=0)
    o_ref[...] = (o_t.T * _silu(ga_ref[...])).astype(BF16)


def _foxdec_step(first, last, q_ref, kt_refs, vt_refs, lf_refs, knew_ref, vnew_ref, cncol_ref, cnrow_ref, ga_ref,
                 o_ref, m_ref, l_ref, acc_ref, carry_ref):
    n = len(kt_refs)
    rows = q_ref.shape[0]
    npos = n * PAGE_SIZE

    @pl.when(first)
    def _():
        m_ref[...] = jnp.full_like(m_ref, NEG_BIG)
        l_ref[...] = jnp.zeros_like(l_ref)
        acc_ref[...] = jnp.zeros_like(acc_ref)
        carry_ref[...] = jnp.zeros_like(carry_ref)

    def accumulate(s, pv):
        m_prev = m_ref[...]
        m_new = jnp.maximum(m_prev, jnp.max(s, axis=1, keepdims=True))
        alpha = jnp.exp2(m_prev - m_new)
        p = jnp.exp2(s - m_new)
        l_ref[...] = alpha * l_ref[...] + jnp.sum(p, axis=1, keepdims=True)
        acc_ref[...] = alpha * acc_ref[...] + pv(p.astype(BF16))
        m_ref[...] = m_new

    row = lax.broadcasted_iota(jnp.int32, (PAGE_SIZE, PAGE_SIZE), 0)
    col = lax.broadcasted_iota(jnp.int32, (PAGE_SIZE, PAGE_SIZE), 1)
    later = jnp.where(row > col, 1.0, 0.0).astype(BF16)
    srev = [None] * n
    for i in reversed(range(n)):
        x = lf_refs[i][...]
        hi, mid, lo = _split3(x)
        srev[i] = _dot(hi, later) + _dot(mid, later) + _dot(lo, later) + carry_ref[...]
        carry_ref[...] = carry_ref[...] + jnp.sum(x, axis=1, keepdims=True)
    srev = jnp.concatenate(srev, axis=1) * LOG2E

    q = q_ref[...]
    cn_col = cncol_ref[...] * LOG2E
    kt = jnp.concatenate([r[...].astype(BF16) for r in kt_refs], axis=1)
    vt = jnp.concatenate([r[...].astype(BF16) for r in vt_refs], axis=1)
    s = _dot(q, kt)
    s = (s.reshape(rows // N_HEADS, N_HEADS, npos) + srev[None]).reshape(rows, npos)
    s = s + jnp.concatenate([cn_col] * n, axis=1)
    accumulate(s, lambda p: _dot_nt(p, vt))

    @pl.when(last)
    def _():
        sn = _dot_nt(q, knew_ref[...]) + (cn_col - cnrow_ref[...] * LOG2E)
        row = lax.broadcasted_iota(jnp.int32, (rows, LANES), 0)
        col = lax.broadcasted_iota(jnp.int32, (rows, LANES), 1)
        sn = jnp.where(col <= lax.shift_right_logical(row, 4), sn, NEG_BIG)
        accumulate(sn, lambda p: _dot(p, vnew_ref[...]))
        o = acc_ref[...] / l_ref[...]
        row = lax.broadcasted_iota(jnp.int32, (rows, D_ATTN), 0)
        col = lax.broadcasted_iota(jnp.int32, (rows, D_ATTN), 1)
        own = (row & (N_HEADS - 1)) == lax.shift_right_logical(col, 6)
        o = jnp.sum(jnp.where(own, o, 0.0).reshape(rows // N_HEADS, N_HEADS, D_ATTN), axis=1)
        o_ref[...] = o * _silu(ga_ref[...])


def _attn_kernel(nq, nchunk, pt_ref, *refs):
    n = PAGES_PER_STEP
    qa_ref, ka_ref, vtp_ref, gap_ref, q_ref = refs[:5]
    kt_refs = refs[5:5 + n]
    vt_refs = refs[5 + n:5 + 2 * n]
    lf_refs = refs[5 + 2 * n:5 + 3 * n]
    (knew_ref, vnew_ref, cncol_ref, cnrow_ref, gas_ref, op_ref, os_ref,
     mp_ref, lp_ref, accp_ref, ms_ref, ls_ref, accs_ref, carry_ref) = refs[5 + 3 * n:]
    g = pl.program_id(0)
    _fox_step(g % nq, qa_ref, ka_ref, vtp_ref, gap_ref, op_ref, mp_ref, lp_ref, accp_ref)
    c = g % nchunk
    _foxdec_step(c == 0, c == nchunk - 1, q_ref, kt_refs, vt_refs, lf_refs, knew_ref, vnew_ref, cncol_ref,
                 cnrow_ref, gas_ref, os_ref, ms_ref, ls_ref, accs_ref, carry_ref)


def _attention(layer, page_table, qa, ka, vt, z_p, batch, seq, qbd, kt_pages, vt_pages, lf_pages, knew, vnew,
               cn_col, cn_row, z_s):
    tq = vt.shape[4]
    nq = seq // tq
    pairs = N_HEADS // 2
    n_dec, n_pages = page_table.shape
    n = PAGES_PER_STEP
    nchunk = n_pages // n
    steps = batch * pairs * nq
    assert steps == n_dec * nchunk, (steps, n_dec, nchunk)
    rows = qbd.shape[1]
    t_len = rows // N_HEADS
    ga_blk = COL_GA // LANES

    def prow(g):
        return (g // (pairs * nq)) * nq + g % nq

    pcol = lambda g: (g // nq) % pairs
    drow = lambda g: g // nchunk
    dchunk = lambda g: nchunk - 1 - g % nchunk

    def page_spec(shape, i):
        return pl.BlockSpec((None, None) + shape, lambda g, pt: (pt[drow(g), dchunk(g) * n + i], layer, 0, 0))

    per_row = lambda shape: pl.BlockSpec((None,) + shape, lambda g, pt: (drow(g), 0, 0))
    grid_spec = pltpu.PrefetchScalarGridSpec(
        num_scalar_prefetch=1,
        grid=(steps,),
        in_specs=([pl.BlockSpec((tq, 2 * LANES), lambda g, pt: (prow(g), pcol(g))),
                   pl.BlockSpec((seq, 2 * LANES), lambda g, pt: (g // (pairs * nq), pcol(g))),
                   pl.BlockSpec((None, None) + vt.shape[2:], lambda g, pt: (g // (pairs * nq), pcol(g), 0, 0, 0)),
                   pl.BlockSpec((tq, LANES), lambda g, pt: (prow(g), ga_blk + pcol(g))),
                   per_row((rows, D_ATTN))]
                  + [page_spec((D_ATTN, PAGE_SIZE), i) for i in range(n)]
                  + [page_spec((D_ATTN, PAGE_SIZE), i) for i in range(n)]
                  + [page_spec((N_HEADS, PAGE_SIZE), i) for i in range(n)]
                  + [per_row((LANES, D_ATTN)), per_row((LANES, D_ATTN)),
                     per_row((rows, LANES)), per_row((rows, LANES)),
                     pl.BlockSpec((t_len, D_ATTN), lambda g, pt: (drow(g), COL_GA // D_ATTN))]),
        out_specs=[pl.BlockSpec((tq, LANES), lambda g, pt: (prow(g), pcol(g))),
                   pl.BlockSpec((t_len, D_ATTN), lambda g, pt: (drow(g), 0))],
        scratch_shapes=[pltpu.VMEM((2, 1, tq), F32), pltpu.VMEM((2, 1, tq), F32), pltpu.VMEM((2, HEAD_DIM, tq), F32),
                        pltpu.VMEM((rows, 1), F32), pltpu.VMEM((rows, 1), F32), pltpu.VMEM((rows, D_ATTN), F32),
                        pltpu.VMEM((N_HEADS, LANES), F32)],
    )
    return pl.pallas_call(
        functools.partial(_attn_kernel, nq, nchunk),
        grid_spec=grid_spec,
        out_shape=[jax.ShapeDtypeStruct((batch * seq, D_ATTN), BF16),
                   jax.ShapeDtypeStruct((n_dec * t_len, D_ATTN), F32)],
        compiler_params=_cparams(("arbitrary",), 56),
        name="attn",
    )(page_table, qa, ka, vt, z_p, qbd, *([kt_pages] * n), *([vt_pages] * n), *([lf_pages] * n),
      knew, vnew, cn_col, cn_row, z_s)


def _s5prep_kernel(lr_ref, li_ref, ldt_ref, lrx_ref, lix_ref, bre_ref, bim_ref,
                   abr_ref, abi_ref, bbr_ref, bbi_ref):
    dt = jnp.exp(ldt_ref[...])

    def discretise(lr, li):
        mag = jnp.exp(lr * dt)
        abr = mag * jnp.cos(li * dt)
        abi = mag * jnp.sin(li * dt)
        den = lr * lr + li * li
        pr = abr - 1.0
        return abr, abi, (pr * lr + abi * li) / den, (abi * lr - pr * li) / den

    abr, abi, _, _ = discretise(lr_ref[...], li_ref[...])
    abr_ref[...] = abr
    abi_ref[...] = abi
    _, _, fr, fi = discretise(lrx_ref[...], lix_ref[...])
    bre, bim = bre_ref[...], bim_ref[...]
    bbr_ref[...] = fr * bre - fi * bim
    bbi_ref[...] = fr * bim + fi * bre


def _s5prep_all(lam_re, lam_im, log_dt, b_re, b_im):
    depth, g, n = lam_re.shape
    c = b_re.shape[-1]
    small = pl.BlockSpec((None, g, n), lambda l: (l, 0, 0))
    wide = pl.BlockSpec((None, g, n * c), lambda l: (l, 0, 0))
    return pl.pallas_call(
        _s5prep_kernel,
        grid=(depth,),
        in_specs=[small, small, pl.BlockSpec((None, g, 1), lambda l: (l, 0, 0)), wide, wide, wide, wide],
        out_specs=[small, small, wide, wide],
        out_shape=[jax.ShapeDtypeStruct((depth, g, n), F32)] * 2 + [jax.ShapeDtypeStruct((depth, g, n * c), F32)] * 2,
        compiler_params=_cparams(("arbitrary",), 32),
        name="s5prep",
    )(lam_re, lam_im, log_dt.reshape(depth, g, 1),
      jnp.repeat(lam_re, c, axis=-1), jnp.repeat(lam_im, c, axis=-1),
      b_re.reshape(depth, g, n * c), b_im.reshape(depth, g, n * c))


def _s5_kernel(x_ref, h0_ref, bset_ref, cset_ref, ar_ref, ai_ref, dsk_ref, wglu_ref, bglu_ref,
               y_ref, hfin_ref, u_ref, hst_ref):
    rows = x_ref.shape[0]
    nb = SUBLANES
    half = SET_H // 2

    @pl.when(pl.program_id(0) == 0)
    def _():
        hst_ref[...] = h0_ref[...]

    x = x_ref[...]
    xb = x.astype(BF16)
    for s in range(S5_SETS):
        u_ref[:, s * SET_H:(s + 1) * SET_H] = _dot(xb[:, s * SET_X:(s + 1) * SET_X], bset_ref[s])

    ar, ai = ar_ref[...], ai_ref[...]

    def step(t, h):
        r0 = pl.multiple_of(t * nb, nb)
        u = u_ref[pl.ds(r0, nb), :]
        out = []
        for s in range(S5_SETS):
            o = s * SET_H
            hr, hi = h[:, o:o + half], h[:, o + half:o + SET_H]
            a_r, a_i = ar[:, s * half:(s + 1) * half], ai[:, s * half:(s + 1) * half]
            out.append(a_r * hr - a_i * hi + u[:, o:o + half])
            out.append(a_r * hi + a_i * hr + u[:, o + half:o + SET_H])
        hn = jnp.concatenate(out, axis=1)
        u_ref[pl.ds(r0, nb), :] = hn
        return hn

    h = lax.fori_loop(0, rows // nb, step, hst_ref[...])
    hst_ref[...] = h
    hfin_ref[...] = h

    hb = u_ref[...].astype(BF16)
    y = jnp.concatenate([_dot(hb[:, s * SET_H:(s + 1) * SET_H], cset_ref[s]) for s in range(S5_SETS)], axis=1)
    y = y + dsk_ref[...] * x
    y = 0.5 * y * (1.0 + jnp.tanh(0.7978845608028654 * (y + 0.044715 * (y * y * y))))
    y_ref[...] = y * jax.nn.sigmoid(_dot(y.astype(BF16), wglu_ref[...]) + bglu_ref[...])


def _s5(x_tb, h0, bset, cset, ar, ai, d_skip, w_glu, b_glu, tt):
    rows = x_tb.shape[0]
    blk = tt * SUBLANES
    const2 = lambda shape: pl.BlockSpec(shape, lambda i: (0, 0))
    const3 = lambda shape: pl.BlockSpec(shape, lambda i: (0, 0, 0))
    return pl.pallas_call(
        _s5_kernel,
        grid=(rows // blk,),
        in_specs=[pl.BlockSpec((blk, D_SSM), lambda i: (i, 0)),
                  const2((SUBLANES, D_STATE)),
                  const3((S5_SETS, SET_X, SET_H)), const3((S5_SETS, SET_H, SET_X)),
                  const2((SUBLANES, D_STATE // 2)), const2((SUBLANES, D_STATE // 2)),
                  const2((1, D_SSM)), const2((D_SSM, D_SSM)), const2((1, D_SSM))],
        out_specs=[pl.BlockSpec((blk, D_SSM), lambda i: (i, 0)), const2((SUBLANES, D_STATE))],
        out_shape=[jax.ShapeDtypeStruct((rows, D_SSM), F32), jax.ShapeDtypeStruct((SUBLANES, D_STATE), F32)],
        scratch_shapes=[pltpu.VMEM((blk, D_STATE), F32), pltpu.VMEM((SUBLANES, D_STATE), F32)],
        compiler_params=_cparams(("arbitrary",), 48),
        name="s5",
    )(x_tb, h0, bset, cset, ar, ai, d_skip, w_glu, b_glu)


def _s5_operands(abr, abi, bbr, bbi, c_re, c_im):
    g, n, c = N_SSM_GROUPS, SSM_STATE, SSM_GROUP
    gs = g // S5_SETS
    eye = jnp.eye(gs, dtype=F32)

    def in_map(bb):
        bb = bb.reshape(S5_SETS, gs, n, c).transpose(0, 1, 3, 2)
        return (bb[:, :, :, None, :] * eye[None, :, None, :, None]).reshape(S5_SETS, gs * c, gs * n)

    def out_map(cc):
        cc = cc.reshape(S5_SETS, gs, c, n).transpose(0, 1, 3, 2)
        return (cc[:, :, :, None, :] * eye[None, :, None, :, None]).reshape(S5_SETS, gs * n, gs * c)

    bset = jnp.concatenate([in_map(bbr), in_map(bbi)], axis=2).astype(BF16)
    cset = jnp.concatenate([out_map(c_re), -out_map(c_im)], axis=1).astype(BF16)
    ar = jnp.broadcast_to(abr.reshape(1, g * n), (SUBLANES, g * n))
    ai = jnp.broadcast_to(abi.reshape(1, g * n), (SUBLANES, g * n))
    return bset, cset, ar, ai


def _state_to_lanes(h_re, h_im):
    nb = h_re.shape[0]
    gs = N_SSM_GROUPS // S5_SETS
    st = jnp.stack([h_re.reshape(nb, S5_SETS, gs, SSM_STATE), h_im.reshape(nb, S5_SETS, gs, SSM_STATE)], axis=2)
    return st.reshape(nb, D_STATE)


def _lanes_to_state(h):
    nb = h.shape[0]
    gs = N_SSM_GROUPS // S5_SETS
    st = h.reshape(nb, S5_SETS, 2, gs, SSM_STATE)
    return (st[:, :, 0].reshape(nb, N_SSM_GROUPS, SSM_STATE), st[:, :, 1].reshape(nb, N_SSM_GROUPS, SSM_STATE))


def _cmlp_kernel(lc, emit_vn, u_ref, v_ref, gm_ref, g_ref, b_ref, w_ref, bias_ref, o_ref, *vn_ref):
    tm = u_ref.shape[0]
    v = v_ref[...]
    mu = jnp.mean(v, axis=-1, keepdims=True)
    d = v - mu
    var = jnp.mean(d * d, axis=-1, keepdims=True)
    vn = d * lax.rsqrt(var + EPS) * g_ref[...] + b_ref[...]
    if emit_vn:
        vn_ref[0][...] = vn
    row = lax.broadcasted_iota(jnp.int32, (lc, lc), 0)
    col = lax.broadcasted_iota(jnp.int32, (lc, lc), 1)
    ug = u_ref[...] * _silu(gm_ref[...])
    vb = vn.astype(BF16)
    for g in range(N_CMLP_GROUPS):
        wg = jnp.where(col <= row, w_ref[g], 0.0).astype(BF16)
        lanes = slice(g * CMLP_GROUP, (g + 1) * CMLP_GROUP)
        for c in range(tm // lc):
            rws = slice(c * lc, (c + 1) * lc)
            zc = _dot(wg, vb[rws, lanes]) + bias_ref[:, lanes]
            o_ref[rws, lanes] = (ug[rws, lanes] * zc).astype(BF16)


def _cmlp(z, ln_g, ln_b, w, bias, tm, lc, emit_vn):
    t = z.shape[0]
    col = lambda c: pl.BlockSpec((tm, D_CMLP), lambda i: (i, c))
    const2 = lambda shape: pl.BlockSpec(shape, lambda i: (0, 0))
    out_specs = [pl.BlockSpec((tm, D_CMLP), lambda i: (i, 0))]
    out_shape = [jax.ShapeDtypeStruct((t, D_CMLP), BF16)]
    if emit_vn:
        out_specs.append(pl.BlockSpec((tm, D_CMLP), lambda i: (i, 0)))
        out_shape.append(jax.ShapeDtypeStruct((t, D_CMLP), F32))
    return pl.pallas_call(
        functools.partial(_cmlp_kernel, lc, emit_vn),
        grid=(t // tm,),
        in_specs=[col(COL_U // D_CMLP), col(COL_VV // D_CMLP), col(COL_GM // D_CMLP),
                  const2((1, D_CMLP)), const2((1, D_CMLP)),
                  pl.BlockSpec((N_CMLP_GROUPS, lc, lc), lambda i: (0, 0, 0)),
                  const2((lc, D_CMLP))],
        out_specs=out_specs,
        out_shape=out_shape,
        compiler_params=_cparams(("arbitrary",), 32),
        name="cmlp",
    )(z, z, z, ln_g, ln_b, w, bias)


def _outproj_kernel(x_ref, a_ref, s_ref, gs_ref, m_ref, gate_ref, w_ref, o_ref):
    sg = (s_ref[...] * _silu(gs_ref[...])).astype(BF16)
    acc = (_dot(a_ref[...].astype(BF16), w_ref[0:D_ATTN, :])
           + _dot(sg, w_ref[D_ATTN:D_ATTN + D_SSM, :])
           + _dot(m_ref[...], w_ref[D_ATTN + D_SSM:, :]))
    o_ref[...] = x_ref[...] + gate_ref[...] * acc


def _outproj(layer, x, a, s, z, m, gate, w_out, tm, rows_per_mod):
    t = x.shape[0]
    mod_rows = gate.shape[1]
    return pl.pallas_call(
        _outproj_kernel,
        grid=(t // tm,),
        in_specs=[pl.BlockSpec((tm, D_MODEL), lambda i: (i, 0)),
                  pl.BlockSpec((tm, D_ATTN), lambda i: (i, 0)),
                  pl.BlockSpec((tm, D_SSM), lambda i: (i, 0)),
                  pl.BlockSpec((tm, D_SSM), lambda i: (i, COL_GS // D_SSM)),
                  pl.BlockSpec((tm, D_CMLP), lambda i: (i, 0)),
                  pl.BlockSpec((None, mod_rows, D_MODEL), lambda i: ((i * tm) // rows_per_mod, 0, 0)),
                  pl.BlockSpec((None, D_MODEL, D_MODEL), lambda i: (layer, 0, 0))],
        out_specs=pl.BlockSpec((tm, D_MODEL), lambda i: (i, 0)),
        out_shape=jax.ShapeDtypeStruct((t, D_MODEL), F32),
        compiler_params=_cparams(("arbitrary",), 56),
        name="outproj",
    )(x, a, s, z, m, gate, w_out)


def _pack_w_in(w_in):
    d_q = 3 * D_ATTN
    wt = w_in.transpose(0, 2, 1)
    main = jnp.concatenate([wt[:, :d_q], wt[:, d_q + N_HEADS:]], axis=1).astype(BF16)
    wt_fg = jnp.pad(wt[:, d_q:d_q + N_HEADS], ((0, 0), (0, LANES - N_HEADS), (0, 0))).astype(BF16)
    return main, wt_fg


def _layer_front(layer, depth, x, mods, wl, consts, tm, seq_len, prompt, caches=None):
    shift, scale, _, rows_per_mod = mods
    z, logf = _inproj(layer, x, wl["norm_g"], scale, shift, wl["wt_main"], wl["wt_fg"], wl["b_f"], tm, rows_per_mod)
    qk = _qkprep(z, logf, wl["qg"], wl["kg"], consts, min(tm, TQ), seq_len, prompt, layer, depth, caches)
    return z, logf, qk


def _layer_back(layer, x, z, logf, qk, a_out, mods, h0_lanes, seq_len, batch, wl, tm, tt, lc, prompt):
    t = x.shape[0]
    gate, rows_per_mod = mods[2], mods[3]
    xs = z[:, COL_XS:COL_XS + D_SSM].reshape(batch, seq_len, D_SSM).transpose(1, 0, 2)
    xs = jnp.pad(xs, ((0, 0), (0, SUBLANES - batch), (0, 0))).reshape(seq_len * SUBLANES, D_SSM)
    s_tb, hfin = _s5(xs, h0_lanes, wl["bset"], wl["cset"], wl["ar"], wl["ai"], wl["d_skip"],
                     wl["w_glu"], wl["b_glu"], tt)
    s_pre = s_tb.reshape(seq_len, SUBLANES, D_SSM)[:, :batch].transpose(1, 0, 2).reshape(t, D_SSM)

    m_res = _cmlp(z, wl["ln_g"], wl["ln_b"], wl["w_s"], wl["b_s"], min(tm, 512), lc, not prompt)
    y = _outproj(layer, x, a_out, s_pre, z, m_res[0], gate, wl["w_out"], min(tm, 512), rows_per_mod)
    h_re, h_im = _lanes_to_state(hfin[:batch])
    if prompt:
        return y, (h_re, h_im)
    _, _, kn, _, _ = qk
    return y, (kn, z[:, COL_V:COL_V + D_ATTN], logf[:, :N_HEADS], h_re, h_im, m_res[1])


def kernel(x_prompt, x_sample, c_prompt, c_sample, cache_k, cache_v, cache_logf, state_ssm_re, state_ssm_im,
           page_table, norm_g, w_ada, b_ada, w_in, b_f, q_norm_g, k_norm_g, lam_re, lam_im, log_dt,
           b_re, b_im, c_re, c_im, d_skip, w_glu, b_glu, sgu_ln_g, sgu_ln_b, w_s, b_s, w_out):
    depth = w_in.shape[0]
    bp, seq, _ = x_prompt.shape
    bd, t_dec, _ = x_sample.shape
    n_pool = cache_k.shape[0]

    c_all = jnp.concatenate([c_prompt, c_sample], axis=0)
    c_all = jnp.pad(c_all, ((0, 2 * SUBLANES - bp - bd), (0, 0)))
    mod = _ada_all(c_all, w_ada, b_ada)

    wt_main, wt_fg = _pack_w_in(w_in)
    w_out_b = w_out.astype(BF16)
    w_glu_b = w_glu.astype(BF16)
    b_f_pad = jnp.pad(b_f, ((0, 0), (0, LANES - N_HEADS))).reshape(depth, 1, LANES)
    consts = _aug_placement()
    abr, abi, bbr, bbi = _s5prep_all(lam_re, lam_im, log_dt, b_re, b_im)

    kt_pages = cache_k.transpose(0, 1, 3, 4, 2).reshape(n_pool, depth, D_ATTN, PAGE_SIZE)
    vt_pages = cache_v.transpose(0, 1, 3, 4, 2).reshape(n_pool, depth, D_ATTN, PAGE_SIZE)
    lf_pages = cache_logf.transpose(0, 1, 3, 2)

    eye_d = jnp.eye(bd, dtype=F32)
    w_s_dec = (eye_d[None, None, :, None, :, None] * w_s[:, :, None, :t_dec, None, :t_dec]).reshape(
        depth, N_CMLP_GROUPS, bd * t_dec, bd * t_dec)
    bias_p = jnp.repeat(b_s.transpose(0, 2, 1), CMLP_GROUP, axis=-1)
    bias_d = jnp.tile(bias_p[:, :t_dec], (1, bd, 1))

    yp = x_prompt.reshape(bp * seq, D_MODEL)
    ys = x_sample.reshape(bd * t_dec, D_MODEL)
    h_zero = jnp.zeros((SUBLANES, D_STATE), F32)
    eye_h = jnp.eye(N_HEADS, dtype=BF16)
    outs_p, outs_s = [], []
    caches_p = None
    for l in range(depth):
        bset, cset, ar, ai = _s5_operands(abr[l], abi[l], bbr[l], bbi[l], c_re[l], c_im[l])
        wl = dict(norm_g=norm_g[l][None], wt_main=wt_main, wt_fg=wt_fg, b_f=b_f_pad[l],
                  qg=jnp.tile(q_norm_g[l], N_HEADS)[None], kg=jnp.tile(k_norm_g[l], N_HEADS)[None],
                  bset=bset, cset=cset, ar=ar, ai=ai, d_skip=d_skip[l][None], w_glu=w_glu_b[l],
                  b_glu=b_glu[l][None], ln_g=sgu_ln_g[l][None], ln_b=sgu_ln_b[l][None], w_out=w_out_b)
        wl_p = dict(wl, w_s=w_s[l], b_s=bias_p[l])
        wl_s = dict(wl, w_s=w_s_dec[l], b_s=bias_d[l])
        shift, scale, gate = (mod[l][:, i * D_MODEL:(i + 1) * D_MODEL] for i in range(3))
        mods_p = (shift[:bp, None], scale[:bp, None], gate[:bp, None], seq)
        rep = lambda m: jnp.repeat(m[bp:bp + bd], t_dec, axis=0)[None]
        mods_s = (rep(shift), rep(scale), rep(gate), bd * t_dec)

        z_p, logf_p, qk_p = _layer_front(l, depth, yp, mods_p, wl_p, consts, 1024, seq, True, caches_p)
        caches_p = (qk_p[2], qk_p[3], qk_p[5])
        z_s, logf_s, qk_s = _layer_front(l, depth, ys, mods_s, wl_s, consts, bd * t_dec, t_dec, False)

        qa, ka, _, vb, f = qk_s
        q = qa.reshape(bd, t_dec, N_HEADS, 2 * HEAD_DIM)[..., :HEAD_DIM]
        qbd = (q[:, :, :, None, :] * eye_h[None, None, :, :, None]).reshape(bd, t_dec * N_HEADS, D_ATTN)
        kb = ka.reshape(bd, t_dec, N_HEADS, 2 * HEAD_DIM)[..., :HEAD_DIM].reshape(bd, t_dec, D_ATTN)
        knew = jnp.pad(kb, ((0, 0), (0, LANES - t_dec), (0, 0)))
        vnew = jnp.pad(vb.reshape(bd, t_dec, D_ATTN), ((0, 0), (0, LANES - t_dec), (0, 0)))
        cn = f[:, :N_HEADS].reshape(bd, t_dec, N_HEADS)
        cn_col = jnp.broadcast_to(cn.reshape(bd, t_dec * N_HEADS, 1), (bd, t_dec * N_HEADS, LANES))
        cn_row = jnp.broadcast_to(cn.transpose(0, 2, 1)[:, None], (bd, t_dec, N_HEADS, t_dec))
        cn_row = jnp.pad(cn_row.reshape(bd, t_dec * N_HEADS, t_dec), ((0, 0), (0, 0), (0, LANES - t_dec)))
        a_p, a_s = _attention(l, page_table, qk_p[0], qk_p[1], qk_p[4], z_p, bp, seq, qbd, kt_pages, vt_pages,
                              lf_pages, knew, vnew, cn_col, cn_row, z_s)

        yp, cache_p = _layer_back(l, yp, z_p, logf_p, qk_p, a_p, mods_p, h_zero, seq, bp, wl_p, 1024, 64, CHUNK, True)
        h0 = _state_to_lanes(state_ssm_re[:, l], state_ssm_im[:, l])
        ys, cache_s = _layer_back(l, ys, z_s, logf_s, qk_s, a_s, mods_s, h0, t_dec, bd, wl_s, bd * t_dec, t_dec,
                                  bd * t_dec, False)
        outs_p.append(cache_p)
        outs_s.append(cache_s)

    def stack(outs, idx, shape):
        return jnp.stack([o[idx].reshape(shape) for o in outs], axis=1)

    knt, vt32, lft = caches_p
    k_prompt = knt.reshape(bp, depth, N_HEADS, HEAD_DIM, seq).transpose(0, 1, 4, 2, 3)
    v_prompt = vt32.reshape(bp, depth, N_HEADS, HEAD_DIM, seq).transpose(0, 1, 4, 2, 3)
    logf_prompt = lft.transpose(0, 1, 3, 2)
    return (yp.reshape(bp, seq, D_MODEL), ys.reshape(bd, t_dec, D_MODEL),
            k_prompt, v_prompt, logf_prompt,
            stack(outs_p, 0, (bp, N_SSM_GROUPS, SSM_STATE)), stack(outs_p, 1, (bp, N_SSM_GROUPS, SSM_STATE)),
            stack(outs_s, 0, (bd, t_dec, N_HEADS, HEAD_DIM)), stack(outs_s, 1, (bd, t_dec, N_HEADS, HEAD_DIM)),
            stack(outs_s, 2, (bd, t_dec, N_HEADS)),
            stack(outs_s, 3, (bd, N_SSM_GROUPS, SSM_STATE)), stack(outs_s, 4, (bd, N_SSM_GROUPS, SSM_STATE)),
            stack(outs_s, 5, (bd, t_dec, D_CMLP)))
```

```python
import functools

import jax
import jax.numpy as jnp
from jax import lax
from jax.experimental import pallas as pl
from jax.experimental.pallas import tpu as pltpu

F32 = jnp.float32
BF16 = jnp.bfloat16

D_MODEL = 2048
HEAD_DIM = 64
D_ATTN = 1024
N_HEADS = 16
D_SSM = 512
SSM_GROUP = 16
N_SSM_GROUPS = 32
SSM_STATE = 64
D_CMLP = 512
CHUNK = 128
N_CMLP_GROUPS = 4
CMLP_GROUP = 128
PAGE_SIZE = 128
EPS = 1e-6
LANES = 128
SUBLANES = 8
NEG_BIG = -1e30
LOG2E = 1.4426950408889634

COL_Q, COL_K, COL_V, COL_GA = 0, 1024, 2048, 3072
COL_XS, COL_GS, COL_U, COL_VV, COL_GM = 4096, 4608, 5120, 5632, 6144
D_Z = 6656
D_STATE = 2 * N_SSM_GROUPS * SSM_STATE
S5_SETS = 2
SET_X = D_SSM // S5_SETS
SET_H = D_STATE // S5_SETS
PAGES_PER_STEP = 8
TQ = 512
INPROJ_TN = 1664


def _cparams(sem, vmem_mb):
    return pltpu.CompilerParams(dimension_semantics=sem, vmem_limit_bytes=vmem_mb * 1024 * 1024)


def _split3(x):
    hi = x.astype(BF16)
    r1 = x - hi.astype(F32)
    mid = r1.astype(BF16)
    lo = (r1 - mid.astype(F32)).astype(BF16)
    return hi, mid, lo


def _dot(a, b):
    return jnp.dot(a, b, preferred_element_type=F32)


def _dot_nt(a, b):
    return lax.dot_general(a, b, (((1,), (1,)), ((), ())), preferred_element_type=F32)


def _silu(x):
    return x * jax.nn.sigmoid(x)


def _ada_kernel(c_ref, w_ref, b_ref, o_ref):
    s = _silu(c_ref[...]).astype(BF16)
    o_ref[...] = _dot(s, w_ref[...].astype(BF16)) + b_ref[...]


def _ada_all(c_all, w_ada, b_ada):
    depth = w_ada.shape[0]
    rows = c_all.shape[0]
    tn = 1024
    return pl.pallas_call(
        _ada_kernel,
        grid=(depth, 3 * D_MODEL // tn),
        in_specs=[
            pl.BlockSpec((rows, D_MODEL), lambda l, j: (0, 0)),
            pl.BlockSpec((None, D_MODEL, tn), lambda l, j: (l, 0, j)),
            pl.BlockSpec((None, 1, tn), lambda l, j: (l, 0, j)),
        ],
        out_specs=pl.BlockSpec((None, rows, tn), lambda l, j: (l, 0, j)),
        out_shape=jax.ShapeDtypeStruct((depth, rows, 3 * D_MODEL), F32),
        compiler_params=_cparams(("arbitrary", "arbitrary"), 48),
        name="ada",
    )(c_all, w_ada, b_ada.reshape(depth, 1, 3 * D_MODEL))


def _inproj_kernel(x_ref, g_ref, sc_ref, sh_ref, wt_ref, wfgt_ref, bf_ref, z_ref, lf_ref, h_ref):
    @pl.when(pl.program_id(1) == 0)
    def _():
        x = x_ref[...]
        ms = jnp.mean(x * x, axis=-1, keepdims=True)
        y = x * lax.rsqrt(ms + EPS) * g_ref[...]
        hb = (y * (1.0 + sc_ref[...]) + sh_ref[...]).astype(BF16)
        h_ref[...] = hb
        fg = _dot_nt(hb, wfgt_ref[...]) + bf_ref[...]
        lf_ref[...] = jnp.minimum(fg, 0.0) - jnp.log1p(jnp.exp(-jnp.abs(fg)))

    z_ref[...] = _dot_nt(h_ref[...], wt_ref[...])


def _inproj(layer, x, norm_g, scale, shift, wt_main, wt_fg, b_f, tm, rows_per_mod):
    t = x.shape[0]
    tn = INPROJ_TN
    mod_rows = scale.shape[1]
    mod_spec = pl.BlockSpec((None, mod_rows, D_MODEL), lambda i, j: ((i * tm) // rows_per_mod, 0, 0))
    return pl.pallas_call(
        _inproj_kernel,
        grid=(t // tm, D_Z // tn),
        in_specs=[
            pl.BlockSpec((tm, D_MODEL), lambda i, j: (i, 0)),
            pl.BlockSpec((1, D_MODEL), lambda i, j: (0, 0)),
            mod_spec,
            mod_spec,
            pl.BlockSpec((None, tn, D_MODEL), lambda i, j: (layer, j, 0)),
            pl.BlockSpec((None, LANES, D_MODEL), lambda i, j: (layer, 0, 0)),
            pl.BlockSpec((1, LANES), lambda i, j: (0, 0)),
        ],
        out_specs=[
            pl.BlockSpec((tm, tn), lambda i, j: (i, j)),
            pl.BlockSpec((tm, LANES), lambda i, j: (i, 0)),
        ],
        out_shape=[jax.ShapeDtypeStruct((t, D_Z), F32), jax.ShapeDtypeStruct((t, LANES), F32)],
        scratch_shapes=[pltpu.VMEM((tm, D_MODEL), BF16)],
        compiler_params=_cparams(("arbitrary", "arbitrary"), 60),
        name="inproj",
    )(x, norm_g, scale, shift, wt_main, wt_fg, b_f)


def _qkprep_kernel(seq_len, prompt, n_alias, q_ref, k_ref, v_ref, lf_ref, qg_ref, kg_ref, seg_ref, pq_ref, pk_ref,
                   *rest):
    qa_ref, ka_ref = rest[n_alias:n_alias + 2]
    rest = rest[n_alias + 2:]
    if prompt:
        knt_ref, vt32_ref, vt16_ref, lft_ref, f_ref, carry_ref = rest
    else:
        kn_ref, vb_ref, f_ref, carry_ref = rest
    tm = q_ref.shape[0]
    pairs = N_HEADS // 2
    row = lax.broadcasted_iota(jnp.int32, (tm, tm), 0)
    col = lax.broadcasted_iota(jnp.int32, (tm, tm), 1)
    tri = col <= row
    if seq_len < tm:
        shift = seq_len.bit_length() - 1
        tri = tri & (lax.shift_right_logical(row, shift) == lax.shift_right_logical(col, shift))
    tri_b = jnp.where(tri, 1.0, 0.0).astype(BF16)
    lf = lf_ref[...]
    hi, mid, lo = _split3(lf)
    f = _dot(tri_b, hi) + _dot(tri_b, mid) + _dot(tri_b, lo)
    if seq_len > tm:
        @pl.when(pl.program_id(0) % (seq_len // tm) == 0)
        def _():
            carry_ref[...] = jnp.zeros_like(carry_ref)

        f = f + carry_ref[...]
        carry_ref[...] = f[tm - 1:tm, :]
    f_ref[...] = f

    def head_norm(x, g):
        ss = _dot((x * x).astype(BF16), seg_ref[...])
        return x * lax.rsqrt(ss * (1.0 / HEAD_DIM) + EPS) * g

    qs = head_norm(q_ref[...], qg_ref[...]) * (HEAD_DIM ** -0.5 * LOG2E)
    kn = head_norm(k_ref[...], kg_ref[...])
    v = v_ref[...]
    if prompt:
        lft_ref[...] = lf.T[0:N_HEADS, :]
        for p in range(pairs):
            cols = slice(p * LANES, (p + 1) * LANES)
            knt_ref[cols, :] = kn[:, cols].T
            vt = v[:, cols].T
            vt32_ref[cols, :] = vt
            vt16_ref[p] = vt.astype(BF16)
    else:
        kn_ref[...] = kn
        vb_ref[...] = v.astype(BF16)

    fh, fm, fl = _split3(f * LOG2E)
    lane = lax.broadcasted_iota(jnp.int32, (tm, LANES), 1)
    parts = jnp.where(lane < 16, fh.astype(F32),
                      jnp.where(lane < 32, pltpu.roll(fm.astype(F32), 16, 1),
                                jnp.where(lane < 48, pltpu.roll(fl.astype(F32), 32, 1),
                                          jnp.where(lane == 48, 1.0, 0.0)))).astype(BF16)
    ext_q = _dot(parts, pq_ref[...])
    ext_k = _dot(parts, pk_ref[...])
    low = lane < HEAD_DIM
    for src, ext, dst in ((qs, ext_q, qa_ref), (kn, ext_k, ka_ref)):
        for p in range(pairs):
            blk = src[:, p * LANES:(p + 1) * LANES]
            e = ext[:, p * LANES:(p + 1) * LANES]
            dst[:, (2 * p) * LANES:(2 * p + 1) * LANES] = jnp.where(low, blk, pltpu.roll(e, 64, 1)).astype(BF16)
            dst[:, (2 * p + 1) * LANES:(2 * p + 2) * LANES] = jnp.where(low, pltpu.roll(blk, 64, 1), e).astype(BF16)


def _aug_placement():
    r = jnp.arange(LANES)[:, None]
    c = jnp.arange(N_HEADS * HEAD_DIM)[None, :]
    rp, rh = r // 16, r % 16
    ch, cc = c // HEAD_DIM, c % HEAD_DIM
    f_rows = (r < 48) & (rh == ch)
    one_row = r == 48
    pq = jnp.where(f_rows & (cc == rp), 1.0, 0.0) + jnp.where(one_row & (cc >= 3) & (cc < 6), 1.0, 0.0)
    pk = jnp.where(f_rows & (cc == rp + 3), -1.0, 0.0) + jnp.where(one_row & (cc < 3), 1.0, 0.0)
    seg = jnp.where(jnp.arange(D_ATTN)[:, None] // HEAD_DIM == jnp.arange(D_ATTN)[None, :] // HEAD_DIM, 1.0, 0.0)
    return pq.astype(BF16), pk.astype(BF16), seg.astype(BF16)


def _qkprep(z, logf, qg, kg, consts, tm, seq_len, prompt, layer=0, depth=1, caches=None):
    t = z.shape[0]
    pq, pk, seg = consts
    col = lambda c: pl.BlockSpec((tm, D_ATTN), lambda i: (i, c))
    const = lambda shape: pl.BlockSpec(shape, lambda i: (0, 0))
    rows = lambda width: pl.BlockSpec((tm, width), lambda i: (i, 0))
    pairs = N_HEADS // 2
    out_specs = [rows(2 * D_ATTN), rows(2 * D_ATTN)]
    out_shape = [jax.ShapeDtypeStruct((t, 2 * D_ATTN), BF16), jax.ShapeDtypeStruct((t, 2 * D_ATTN), BF16)]
    if prompt:
        nk = seq_len // tm
        batch = t // seq_len
        tspec = lambda width: pl.BlockSpec((None, None, width, tm), lambda i: (i // nk, layer, 0, i % nk))
        out_specs += [tspec(D_ATTN), tspec(D_ATTN),
                      pl.BlockSpec((None, pairs, None, LANES, tm), lambda i: (i // nk, 0, i % nk, 0, 0)),
                      tspec(N_HEADS)]
        out_shape += [jax.ShapeDtypeStruct((batch, depth, D_ATTN, seq_len), F32),
                      jax.ShapeDtypeStruct((batch, depth, D_ATTN, seq_len), F32),
                      jax.ShapeDtypeStruct((batch, pairs, nk, LANES, tm), BF16),
                      jax.ShapeDtypeStruct((batch, depth, N_HEADS, seq_len), F32)]
    else:
        out_specs += [rows(D_ATTN), rows(D_ATTN)]
        out_shape += [jax.ShapeDtypeStruct((t, D_ATTN), F32), jax.ShapeDtypeStruct((t, D_ATTN), BF16)]
    out_specs.append(rows(LANES))
    out_shape.append(jax.ShapeDtypeStruct((t, LANES), F32))
    in_specs = [col(COL_Q // D_ATTN), col(COL_K // D_ATTN), col(COL_V // D_ATTN), rows(LANES),
                const((1, D_ATTN)), const((1, D_ATTN)), const((D_ATTN, D_ATTN)),
                const((LANES, D_ATTN)), const((LANES, D_ATTN))]
    caches = tuple(caches or ())
    aliases = {}
    if caches:
        aliases = {len(in_specs): 2, len(in_specs) + 1: 3, len(in_specs) + 2: 5}
        in_specs = in_specs + [pl.BlockSpec(memory_space=pl.ANY)] * len(caches)
    return pl.pallas_call(
        functools.partial(_qkprep_kernel, seq_len, prompt, len(caches)),
        grid=(t // tm,),
        in_specs=in_specs,
        out_specs=out_specs,
        out_shape=out_shape,
        scratch_shapes=[pltpu.VMEM((1, LANES), F32)],
        input_output_aliases=aliases,
        compiler_params=_cparams(("arbitrary",), 56),
        name="qkprep",
    )(z, z, z, logf, qg, kg, seg, pq, pk, *caches)


def _fox_step(i, qa_ref, ka_ref, vt_ref, ga_ref, o_ref, m_ref, l_ref, acc_ref):
    tq = qa_ref.shape[0]
    tk = vt_ref.shape[2]
    m_ref[...] = jnp.full_like(m_ref, NEG_BIG)
    l_ref[...] = jnp.zeros_like(l_ref)
    acc_ref[...] = jnp.zeros_like(acc_ref)

    def tile(j, masked):
        k0 = pl.multiple_of(j * tk, tk)
        for hh in range(2):
            k = ka_ref[pl.ds(k0, tk), hh * LANES:(hh + 1) * LANES]
            s = _dot_nt(k, qa_ref[:, hh * LANES:(hh + 1) * LANES])
            if masked:
                row = lax.broadcasted_iota(jnp.int32, (tk, tq), 0)
                col = lax.broadcasted_iota(jnp.int32, (tk, tq), 1)
                s = jnp.where(row <= col, s, NEG_BIG)
            m_prev = m_ref[hh]
            m_new = jnp.maximum(m_prev, jnp.max(s, axis=0, keepdims=True))
            alpha = jnp.exp2(m_prev - m_new)
            p = jnp.exp2(s - m_new)
            l_ref[hh] = alpha * l_ref[hh] + jnp.sum(p, axis=0, keepdims=True)
            vt = vt_ref[j, hh * HEAD_DIM:(hh + 1) * HEAD_DIM, :]
            acc_ref[hh] = alpha * acc_ref[hh] + _dot(vt, p.astype(BF16))
            m_ref[hh] = m_new

    def body(j, carry):
        tile(j, False)
        return carry

    lax.fori_loop(0, i, body, 0)
    tile(i, True)
    o_t = jnp.concatenate([acc_ref[0] / l_ref[0], acc_ref[1] / l_ref[1]], axis=0)
    o_ref[...] = (o_t.T * _silu(ga_ref[...])).astype(BF16)


def _foxdec_step(first, last, q_ref, kt_refs, vt_refs, lf_refs, knew_ref, vnew_ref, cncol_ref, cnrow_ref, ga_ref,
                 o_ref, m_ref, l_ref, acc_ref, carry_ref):
    n = len(kt_refs)
    rows = q_ref.shape[0]
    npos = n * PAGE_SIZE

    @pl.when(first)
    def _():
        m_ref[...] = jnp.full_like(m_ref, NEG_BIG)
        l_ref[...] = jnp.zeros_like(l_ref)
        acc_ref[...] = jnp.zeros_like(acc_ref)
        carry_ref[...] = jnp.zeros_like(carry_ref)

    def accumulate(s, pv):
        m_prev = m_ref[...]
        m_new = jnp.maximum(m_prev, jnp.max(s, axis=1, keepdims=True))
        alpha = jnp.exp2(m_prev - m_new)
        p = jnp.exp2(s - m_new)
        l_ref[...] = alpha * l_ref[...] + jnp.sum(p, axis=1, keepdims=True)
        acc_ref[...] = alpha * acc_ref[...] + pv(p.astype(BF16))
        m_ref[...] = m_new

    row = lax.broadcasted_iota(jnp.int32, (PAGE_SIZE, PAGE_SIZE), 0)
    col = lax.broadcasted_iota(jnp.int32, (PAGE_SIZE, PAGE_SIZE), 1)
    later = jnp.where(row > col, 1.0, 0.0).astype(BF16)
    srev = [None] * n
    for i in reversed(range(n)):
        x = lf_refs[i][...]
        hi, mid, lo = _split3(x)
        srev[i] = _dot(hi, later) + _dot(mid, later) + _dot(lo, later) + carry_ref[...]
        carry_ref[...] = carry_ref[...] + jnp.sum(x, axis=1, keepdims=True)
    srev = jnp.concatenate(srev, axis=1) * LOG2E

    q = q_ref[...]
    cn_col = cncol_ref[...] * LOG2E
    kt = jnp.concatenate([r[...].astype(BF16) for r in kt_refs], axis=1)
    vt = jnp.concatenate([r[...].astype(BF16) for r in vt_refs], axis=1)
    s = _dot(q, kt)
    s = (s.reshape(rows // N_HEADS, N_HEADS, npos) + srev[None]).reshape(rows, npos)
    s = s + jnp.concatenate([cn_col] * n, axis=1)
    accumulate(s, lambda p: _dot_nt(p, vt))

    @pl.when(last)
    def _():
        sn = _dot_nt(q, knew_ref[...]) + (cn_col - cnrow_ref[...] * LOG2E)
        row = lax.broadcasted_iota(jnp.int32, (rows, LANES), 0)
        col = lax.broadcasted_iota(jnp.int32, (rows, LANES), 1)
        sn = jnp.where(col <= lax.shift_right_logical(row, 4), sn, NEG_BIG)
        accumulate(sn, lambda p: _dot(p, vnew_ref[...]))
        o = acc_ref[...] / l_ref[...]
        row = lax.broadcasted_iota(jnp.int32, (rows, D_ATTN), 0)
        col = lax.broadcasted_iota(jnp.int32, (rows, D_ATTN), 1)
        own = (row & (N_HEADS - 1)) == lax.shift_right_logical(col, 6)
        o = jnp.sum(jnp.where(own, o, 0.0).reshape(rows // N_HEADS, N_HEADS, D_ATTN), axis=1)
        o_ref[...] = o * _silu(ga_ref[...])


def _attn_kernel(nq, nchunk, pt_ref, *refs):
    n = PAGES_PER_STEP
    qa_ref, ka_ref, vtp_ref, gap_ref, q_ref = refs[:5]
    kt_refs = refs[5:5 + n]
    vt_refs = refs[5 + n:5 + 2 * n]
    lf_refs = refs[5 + 2 * n:5 + 3 * n]
    (knew_ref, vnew_ref, cncol_ref, cnrow_ref, gas_ref, op_ref, os_ref,
     mp_ref, lp_ref, accp_ref, ms_ref, ls_ref, accs_ref, carry_ref) = refs[5 + 3 * n:]
    g = pl.program_id(0)
    _fox_step(g % nq, qa_ref, ka_ref, vtp_ref, gap_ref, op_ref, mp_ref, lp_ref, accp_ref)
    c = g % nchunk
    _foxdec_step(c == 0, c == nchunk - 1, q_ref, kt_refs, vt_refs, lf_refs, knew_ref, vnew_ref, cncol_ref,
                 cnrow_ref, gas_ref, os_ref, ms_ref, ls_ref, accs_ref, carry_ref)


def _attention(layer, page_table, qa, ka, vt, z_p, batch, seq, qbd, kt_pages, vt_pages, lf_pages, knew, vnew,
               cn_col, cn_row, z_s):
    tq = vt.shape[4]
    nq = seq // tq
    pairs = N_HEADS // 2
    n_dec, n_pages = page_table.shape
    n = PAGES_PER_STEP
    nchunk = n_pages // n
    steps = batch * pairs * nq
    assert steps == n_dec * nchunk, (steps, n_dec, nchunk)
    rows = qbd.shape[1]
    t_len = rows // N_HEADS
    ga_blk = COL_GA // LANES

    def prow(g):
        return (g // (pairs * nq)) * nq + g % nq

    pcol = lambda g: (g // nq) % pairs
    drow = lambda g: g // nchunk
    dchunk = lambda g: nchunk - 1 - g % nchunk

    def page_spec(shape, i):
        return pl.BlockSpec((None, None) + shape, lambda g, pt: (pt[drow(g), dchunk(g) * n + i], layer, 0, 0))

    per_row = lambda shape: pl.BlockSpec((None,) + shape, lambda g, pt: (drow(g), 0, 0))
    grid_spec = pltpu.PrefetchScalarGridSpec(
        num_scalar_prefetch=1,
        grid=(steps,),
        in_specs=([pl.BlockSpec((tq, 2 * LANES), lambda g, pt: (prow(g), pcol(g))),
                   pl.BlockSpec((seq, 2 * LANES), lambda g, pt: (g // (pairs * nq), pcol(g))),
                   pl.BlockSpec((None, None) + vt.shape[2:], lambda g, pt: (g // (pairs * nq), pcol(g), 0, 0, 0)),
                   pl.BlockSpec((tq, LANES), lambda g, pt: (prow(g), ga_blk + pcol(g))),
                   per_row((rows, D_ATTN))]
                  + [page_spec((D_ATTN, PAGE_SIZE), i) for i in range(n)]
                  + [page_spec((D_ATTN, PAGE_SIZE), i) for i in range(n)]
                  + [page_spec((N_HEADS, PAGE_SIZE), i) for i in range(n)]
                  + [per_row((LANES, D_ATTN)), per_row((LANES, D_ATTN)),
                     per_row((rows, LANES)), per_row((rows, LANES)),
                     pl.BlockSpec((t_len, D_ATTN), lambda g, pt: (drow(g), COL_GA // D_ATTN))]),
        out_specs=[pl.BlockSpec((tq, LANES), lambda g, pt: (prow(g), pcol(g))),
                   pl.BlockSpec((t_len, D_ATTN), lambda g, pt: (drow(g), 0))],
        scratch_shapes=[pltpu.VMEM((2, 1, tq), F32), pltpu.VMEM((2, 1, tq), F32), pltpu.VMEM((2, HEAD_DIM, tq), F32),
                        pltpu.VMEM((rows, 1), F32), pltpu.VMEM((rows, 1), F32), pltpu.VMEM((rows, D_ATTN), F32),
                        pltpu.VMEM((N_HEADS, LANES), F32)],
    )
    return pl.pallas_call(
        functools.partial(_attn_kernel, nq, nchunk),
        grid_spec=grid_spec,
        out_shape=[jax.ShapeDtypeStruct((batch * seq, D_ATTN), BF16),
                   jax.ShapeDtypeStruct((n_dec * t_len, D_ATTN), F32)],
        compiler_params=_cparams(("arbitrary",), 56),
        name="attn",
    )(page_table, qa, ka, vt, z_p, qbd, *([kt_pages] * n), *([vt_pages] * n), *([lf_pages] * n),
      knew, vnew, cn_col, cn_row, z_s)


def _s5prep_kernel(lr_ref, li_ref, ldt_ref, lrx_ref, lix_ref, bre_ref, bim_ref,
                   abr_ref, abi_ref, bbr_ref, bbi_ref):
    dt = jnp.exp(ldt_ref[...])

    def discretise(lr, li):
        mag = jnp.exp(lr * dt)
        abr = mag * jnp.cos(li * dt)
        abi = mag * jnp.sin(li * dt)
        den = lr * lr + li * li
        pr = abr - 1.0
        return abr, abi, (pr * lr + abi * li) / den, (abi * lr - pr * li) / den

    abr, abi, _, _ = discretise(lr_ref[...], li_ref[...])
    abr_ref[...] = abr
    abi_ref[...] = abi
    _, _, fr, fi = discretise(lrx_ref[...], lix_ref[...])
    bre, bim = bre_ref[...], bim_ref[...]
    bbr_ref[...] = fr * bre - fi * bim
    bbi_ref[...] = fr * bim + fi * bre


def _s5prep_all(lam_re, lam_im, log_dt, b_re, b_im):
    depth, g, n = lam_re.shape
    c = b_re.shape[-1]
    small = pl.BlockSpec((None, g, n), lambda l: (l, 0, 0))
    wide = pl.BlockSpec((None, g, n * c), lambda l: (l, 0, 0))
    return pl.pallas_call(
        _s5prep_kernel,
        grid=(depth,),
        in_specs=[small, small, pl.BlockSpec((None, g, 1), lambda l: (l, 0, 0)), wide, wide, wide, wide],
        out_specs=[small, small, wide, wide],
        out_shape=[jax.ShapeDtypeStruct((depth, g, n), F32)] * 2 + [jax.ShapeDtypeStruct((depth, g, n * c), F32)] * 2,
        compiler_params=_cparams(("arbitrary",), 32),
        name="s5prep",
    )(lam_re, lam_im, log_dt.reshape(depth, g, 1),
      jnp.repeat(lam_re, c, axis=-1), jnp.repeat(lam_im, c, axis=-1),
      b_re.reshape(depth, g, n * c), b_im.reshape(depth, g, n * c))


def _s5_kernel(x_ref, h0_ref, bset_ref, cset_ref, ar_ref, ai_ref, dsk_ref, wglu_ref, bglu_ref,
               y_ref, hfin_ref, u_ref, hst_ref):
    n_seq, tt, _ = x_ref.shape
    nb = SUBLANES
    rows = tt * nb
    rows_in = n_seq * tt
    half = SET_H // 2

    @pl.when(pl.program_id(0) == 0)
    def _():
        hst_ref[...] = h0_ref[...]

    tshift = tt.bit_length() - 1
    r = lax.broadcasted_iota(jnp.int32, (rows, rows_in), 0)
    c = lax.broadcasted_iota(jnp.int32, (rows, rows_in), 1)
    to_scan = jnp.where((lax.shift_right_logical(r, 3) == (c & (tt - 1)))
                        & ((r & (nb - 1)) == lax.shift_right_logical(c, tshift)), 1.0, 0.0).astype(BF16)
    r = lax.broadcasted_iota(jnp.int32, (rows_in, rows), 0)
    c = lax.broadcasted_iota(jnp.int32, (rows_in, rows), 1)
    from_scan = jnp.where((lax.shift_right_logical(c, 3) == (r & (tt - 1)))
                          & ((c & (nb - 1)) == lax.shift_right_logical(r, tshift)), 1.0, 0.0).astype(BF16)

    def reorder(perm, v):
        hi = v.astype(BF16)
        lo = (v - hi.astype(F32)).astype(BF16)
        return _dot(perm, hi) + _dot(perm, lo)

    x = reorder(to_scan, x_ref[...].reshape(rows_in, D_SSM))
    xb = x.astype(BF16)
    for s in range(S5_SETS):
        u_ref[:, s * SET_H:(s + 1) * SET_H] = _dot(xb[:, s * SET_X:(s + 1) * SET_X], bset_ref[s])

    ar, ai = ar_ref[...], ai_ref[...]

    def step(t, h):
        r0 = pl.multiple_of(t * nb, nb)
        u = u_ref[pl.ds(r0, nb), :]
        out = []
        for s in range(S5_SETS):
            o = s * SET_H
            hr, hi = h[:, o:o + half], h[:, o + half:o + SET_H]
            a_r, a_i = ar[:, s * half:(s + 1) * half], ai[:, s * half:(s + 1) * half]
            out.append(a_r * hr - a_i * hi + u[:, o:o + half])
            out.append(a_r * hi + a_i * hr + u[:, o + half:o + SET_H])
        hn = jnp.concatenate(out, axis=1)
        u_ref[pl.ds(r0, nb), :] = hn
        return hn

    h = lax.fori_loop(0, rows // nb, step, hst_ref[...])
    hst_ref[...] = h
    hfin_ref[...] = h

    hb = u_ref[...].astype(BF16)
    y = jnp.concatenate([_dot(hb[:, s * SET_H:(s + 1) * SET_H], cset_ref[s]) for s in range(S5_SETS)], axis=1)
    y = y + dsk_ref[...] * x
    y = 0.5 * y * (1.0 + jnp.tanh(0.7978845608028654 * (y + 0.044715 * (y * y * y))))
    y = y * jax.nn.sigmoid(_dot(y.astype(BF16), wglu_ref[...]) + bglu_ref[...])
    y_ref[...] = reorder(from_scan, y).reshape(n_seq, tt, D_SSM)


def _s5(z3, h0, bset, cset, ar, ai, d_skip, w_glu, b_glu, tt):
    n_seq, seq, _ = z3.shape
    blk = tt * SUBLANES
    const2 = lambda shape: pl.BlockSpec(shape, lambda i: (0, 0))
    const3 = lambda shape: pl.BlockSpec(shape, lambda i: (0, 0, 0))
    return pl.pallas_call(
        _s5_kernel,
        grid=(seq // tt,),
        in_specs=[pl.BlockSpec((n_seq, tt, D_SSM), lambda i: (0, i, COL_XS // D_SSM)),
                  const2((SUBLANES, D_STATE)),
                  const3((S5_SETS, SET_X, SET_H)), const3((S5_SETS, SET_H, SET_X)),
                  const2((SUBLANES, D_STATE // 2)), const2((SUBLANES, D_STATE // 2)),
                  const2((1, D_SSM)), const2((D_SSM, D_SSM)), const2((1, D_SSM))],
        out_specs=[pl.BlockSpec((n_seq, tt, D_SSM), lambda i: (0, i, 0)), const2((SUBLANES, D_STATE))],
        out_shape=[jax.ShapeDtypeStruct((n_seq, seq, D_SSM), F32), jax.ShapeDtypeStruct((SUBLANES, D_STATE), F32)],
        scratch_shapes=[pltpu.VMEM((blk, D_STATE), F32), pltpu.VMEM((SUBLANES, D_STATE), F32)],
        compiler_params=_cparams(("arbitrary",), 48),
        name="s5",
    )(z3, h0, bset, cset, ar, ai, d_skip, w_glu, b_glu)


def _s5_operands(abr, abi, bbr, bbi, c_re, c_im):
    g, n, c = N_SSM_GROUPS, SSM_STATE, SSM_GROUP
    gs = g // S5_SETS
    eye = jnp.eye(gs, dtype=F32)

    def in_map(bb):
        bb = bb.reshape(S5_SETS, gs, n, c).transpose(0, 1, 3, 2)
        return (bb[:, :, :, None, :] * eye[None, :, None, :, None]).reshape(S5_SETS, gs * c, gs * n)

    def out_map(cc):
        cc = cc.reshape(S5_SETS, gs, c, n).transpose(0, 1, 3, 2)
        return (cc[:, :, :, None, :] * eye[None, :, None, :, None]).reshape(S5_SETS, gs * n, gs * c)

    bset = jnp.concatenate([in_map(bbr), in_map(bbi)], axis=2).astype(BF16)
    cset = jnp.concatenate([out_map(c_re), -out_map(c_im)], axis=1).astype(BF16)
    ar = jnp.broadcast_to(abr.reshape(1, g * n), (SUBLANES, g * n))
    ai = jnp.broadcast_to(abi.reshape(1, g * n), (SUBLANES, g * n))
    return bset, cset, ar, ai


def _state_to_lanes(h_re, h_im):
    nb = h_re.shape[0]
    gs = N_SSM_GROUPS // S5_SETS
    st = jnp.stack([h_re.reshape(nb, S5_SETS, gs, SSM_STATE), h_im.reshape(nb, S5_SETS, gs, SSM_STATE)], axis=2)
    return st.reshape(nb, D_STATE)


def _lanes_to_state(h):
    nb = h.shape[0]
    gs = N_SSM_GROUPS // S5_SETS
    st = h.reshape(nb, S5_SETS, 2, gs, SSM_STATE)
    return (st[:, :, 0].reshape(nb, N_SSM_GROUPS, SSM_STATE), st[:, :, 1].reshape(nb, N_SSM_GROUPS, SSM_STATE))


def _cmlp_kernel(lc, emit_vn, u_ref, v_ref, gm_ref, g_ref, b_ref, w_ref, bias_ref, o_ref, *vn_ref):
    tm = u_ref.shape[0]
    v = v_ref[...]
    mu = jnp.mean(v, axis=-1, keepdims=True)
    d = v - mu
    var = jnp.mean(d * d, axis=-1, keepdims=True)
    vn = d * lax.rsqrt(var + EPS) * g_ref[...] + b_ref[...]
    if emit_vn:
        vn_ref[0][...] = vn
    row = lax.broadcasted_iota(jnp.int32, (lc, lc), 0)
    col = lax.broadcasted_iota(jnp.int32, (lc, lc), 1)
    ug = u_ref[...] * _silu(gm_ref[...])
    vb = vn.astype(BF16)
    for g in range(N_CMLP_GROUPS):
        wg = jnp.where(col <= row, w_ref[g], 0.0).astype(BF16)
        lanes = slice(g * CMLP_GROUP, (g + 1) * CMLP_GROUP)
        for c in range(tm // lc):
            rws = slice(c * lc, (c + 1) * lc)
            zc = _dot(wg, vb[rws, lanes]) + bias_ref[:, lanes]
            o_ref[rws, lanes] = (ug[rws, lanes] * zc).astype(BF16)


def _cmlp(z, ln_g, ln_b, w, bias, tm, lc, emit_vn):
    t = z.shape[0]
    col = lambda c: pl.BlockSpec((tm, D_CMLP), lambda i: (i, c))
    const2 = lambda shape: pl.BlockSpec(shape, lambda i: (0, 0))
    out_specs = [pl.BlockSpec((tm, D_CMLP), lambda i: (i, 0))]
    out_shape = [jax.ShapeDtypeStruct((t, D_CMLP), BF16)]
    if emit_vn:
        out_specs.append(pl.BlockSpec((tm, D_CMLP), lambda i: (i, 0)))
        out_shape.append(jax.ShapeDtypeStruct((t, D_CMLP), F32))
    return pl.pallas_call(
        functools.partial(_cmlp_kernel, lc, emit_vn),
        grid=(t // tm,),
        in_specs=[col(COL_U // D_CMLP), col(COL_VV // D_CMLP), col(COL_GM // D_CMLP),
                  const2((1, D_CMLP)), const2((1, D_CMLP)),
                  pl.BlockSpec((N_CMLP_GROUPS, lc, lc), lambda i: (0, 0, 0)),
                  const2((lc, D_CMLP))],
        out_specs=out_specs,
        out_shape=out_shape,
        compiler_params=_cparams(("arbitrary",), 32),
        name="cmlp",
    )(z, z, z, ln_g, ln_b, w, bias)


def _outproj_kernel(x_ref, a_ref, s_ref, gs_ref, m_ref, gate_ref, w_ref, o_ref):
    sg = (s_ref[...] * _silu(gs_ref[...])).astype(BF16)
    acc = (_dot(a_ref[...].astype(BF16), w_ref[0:D_ATTN, :])
           + _dot(sg, w_ref[D_ATTN:D_ATTN + D_SSM, :])
           + _dot(m_ref[...], w_ref[D_ATTN + D_SSM:, :]))
    o_ref[...] = x_ref[...] + gate_ref[...] * acc


def _outproj(layer, x, a, s, z, m, gate, w_out, tm, rows_per_mod):
    t = x.shape[0]
    mod_rows = gate.shape[1]
    return pl.pallas_call(
        _outproj_kernel,
        grid=(t // tm,),
        in_specs=[pl.BlockSpec((tm, D_MODEL), lambda i: (i, 0)),
                  pl.BlockSpec((tm, D_ATTN), lambda i: (i, 0)),
                  pl.BlockSpec((tm, D_SSM), lambda i: (i, 0)),
                  pl.BlockSpec((tm, D_SSM), lambda i: (i, COL_GS // D_SSM)),
                  pl.BlockSpec((tm, D_CMLP), lambda i: (i, 0)),
                  pl.BlockSpec((None, mod_rows, D_MODEL), lambda i: ((i * tm) // rows_per_mod, 0, 0)),
                  pl.BlockSpec((None, D_MODEL, D_MODEL), lambda i: (layer, 0, 0))],
        out_specs=pl.BlockSpec((tm, D_MODEL), lambda i: (i, 0)),
        out_shape=jax.ShapeDtypeStruct((t, D_MODEL), F32),
        compiler_params=_cparams(("arbitrary",), 56),
        name="outproj",
    )(x, a, s, z, m, gate, w_out)


def _pack_w_in(w_in):
    d_q = 3 * D_ATTN
    wt = w_in.transpose(0, 2, 1)
    main = jnp.concatenate([wt[:, :d_q], wt[:, d_q + N_HEADS:]], axis=1).astype(BF16)
    wt_fg = jnp.pad(wt[:, d_q:d_q + N_HEADS], ((0, 0), (0, LANES - N_HEADS), (0, 0))).astype(BF16)
    return main, wt_fg


def _layer_front(layer, depth, x, mods, wl, consts, tm, seq_len, prompt, caches=None):
    shift, scale, _, rows_per_mod = mods
    z, logf = _inproj(layer, x, wl["norm_g"], scale, shift, wl["wt_main"], wl["wt_fg"], wl["b_f"], tm, rows_per_mod)
    qk = _qkprep(z, logf, wl["qg"], wl["kg"], consts, min(tm, TQ), seq_len, prompt, layer, depth, caches)
    return z, logf, qk


def _layer_back(layer, x, z, logf, qk, a_out, mods, h0_lanes, seq_len, batch, wl, tm, tt, lc, prompt):
    t = x.shape[0]
    gate, rows_per_mod = mods[2], mods[3]
    s_pre, hfin = _s5(z.reshape(batch, seq_len, D_Z), h0_lanes, wl["bset"], wl["cset"], wl["ar"], wl["ai"],
                      wl["d_skip"], wl["w_glu"], wl["b_glu"], tt)
    s_pre = s_pre.reshape(t, D_SSM)

    m_res = _cmlp(z, wl["ln_g"], wl["ln_b"], wl["w_s"], wl["b_s"], min(tm, 512), lc, not prompt)
    y = _outproj(layer, x, a_out, s_pre, z, m_res[0], gate, wl["w_out"], min(tm, 512), rows_per_mod)
    h_re, h_im = _lanes_to_state(hfin[:batch])
    if prompt:
        return y, (h_re, h_im)
    _, _, kn, _, _ = qk
    return y, (kn, z[:, COL_V:COL_V + D_ATTN], logf[:, :N_HEADS], h_re, h_im, m_res[1])


def kernel(x_prompt, x_sample, c_prompt, c_sample, cache_k, cache_v, cache_logf, state_ssm_re, state_ssm_im,
           page_table, norm_g, w_ada, b_ada, w_in, b_f, q_norm_g, k_norm_g, lam_re, lam_im, log_dt,
           b_re, b_im, c_re, c_im, d_skip, w_glu, b_glu, sgu_ln_g, sgu_ln_b, w_s, b_s, w_out):
    depth = w_in.shape[0]
    bp, seq, _ = x_prompt.shape
    bd, t_dec, _ = x_sample.shape
    n_pool = cache_k.shape[0]

    c_all = jnp.concatenate([c_prompt, c_sample], axis=0)
    c_all = jnp.pad(c_all, ((0, 2 * SUBLANES - bp - bd), (0, 0)))
    mod = _ada_all(c_all, w_ada, b_ada)

    wt_main, wt_fg = _pack_w_in(w_in)
    w_out_b = w_out.astype(BF16)
    w_glu_b = w_glu.astype(BF16)
    b_f_pad = jnp.pad(b_f, ((0, 0), (0, LANES - N_HEADS))).reshape(depth, 1, LANES)
    consts = _aug_placement()
    abr, abi, bbr, bbi = _s5prep_all(lam_re, lam_im, log_dt, b_re, b_im)

    kt_pages = cache_k.transpose(0, 1, 3, 4, 2).reshape(n_pool, depth, D_ATTN, PAGE_SIZE)
    vt_pages = cache_v.transpose(0, 1, 3, 4, 2).reshape(n_pool, depth, D_ATTN, PAGE_SIZE)
    lf_pages = cache_logf.transpose(0, 1, 3, 2)

    eye_d = jnp.eye(bd, dtype=F32)
    w_s_dec = (eye_d[None, None, :, None, :, None] * w_s[:, :, None, :t_dec, None, :t_dec]).reshape(
        depth, N_CMLP_GROUPS, bd * t_dec, bd * t_dec)
    bias_p = jnp.repeat(b_s.transpose(0, 2, 1), CMLP_GROUP, axis=-1)
    bias_d = jnp.tile(bias_p[:, :t_dec], (1, bd, 1))

    yp = x_prompt.reshape(bp * seq, D_MODEL)
    ys = x_sample.reshape(bd * t_dec, D_MODEL)
    h_zero = jnp.zeros((SUBLANES, D_STATE), F32)
    eye_h = jnp.eye(N_HEADS, dtype=BF16)
    outs_p, outs_s = [], []
    caches_p = None
    for l in range(depth):
        bset, cset, ar, ai = _s5_operands(abr[l], abi[l], bbr[l], bbi[l], c_re[l], c_im[l])
        wl = dict(norm_g=norm_g[l][None], wt_main=wt_main, wt_fg=wt_fg, b_f=b_f_pad[l],
                  qg=jnp.tile(q_norm_g[l], N_HEADS)[None], kg=jnp.tile(k_norm_g[l], N_HEADS)[None],
                  bset=bset, cset=cset, ar=ar, ai=ai, d_skip=d_skip[l][None], w_glu=w_glu_b[l],
                  b_glu=b_glu[l][None], ln_g=sgu_ln_g[l][None], ln_b=sgu_ln_b[l][None], w_out=w_out_b)
        wl_p = dict(wl, w_s=w_s[l], b_s=bias_p[l])
        wl_s = dict(wl, w_s=w_s_dec[l], b_s=bias_d[l])
        shift, scale, gate = (mod[l][:, i * D_MODEL:(i + 1) * D_MODEL] for i in range(3))
        mods_p = (shift[:bp, None], scale[:bp, None], gate[:bp, None], seq)
        rep = lambda m: jnp.repeat(m[bp:bp + bd], t_dec, axis=0)[None]
        mods_s = (rep(shift), rep(scale), rep(gate), bd * t_dec)

        z_p, logf_p, qk_p = _layer_front(l, depth, yp, mods_p, wl_p, consts, 1024, seq, True, caches_p)
        caches_p = (qk_p[2], qk_p[3], qk_p[5])
        z_s, logf_s, qk_s = _layer_front(l, depth, ys, mods_s, wl_s, consts, bd * t_dec, t_dec, False)

        qa, ka, _, vb, f = qk_s
        q = qa.reshape(bd, t_dec, N_HEADS, 2 * HEAD_DIM)[..., :HEAD_DIM]
        qbd = (q[:, :, :, None, :] * eye_h[None, None, :, :, None]).reshape(bd, t_dec * N_HEADS, D_ATTN)
        kb = ka.reshape(bd, t_dec, N_HEADS, 2 * HEAD_DIM)[..., :HEAD_DIM].reshape(bd, t_dec, D_ATTN)
        knew = jnp.pad(kb, ((0, 0), (0, LANES - t_dec), (0, 0)))
        vnew = jnp.pad(vb.reshape(bd, t_dec, D_ATTN), ((0, 0), (0, LANES - t_dec), (0, 0)))
        cn = f[:, :N_HEADS].reshape(bd, t_dec, N_HEADS)
        cn_col = jnp.broadcast_to(cn.reshape(bd, t_dec * N_HEADS, 1), (bd, t_dec * N_HEADS, LANES))
        cn_row = jnp.broadcast_to(cn.transpose(0, 2, 1)[:, None], (bd, t_dec, N_HEADS, t_dec))
        cn_row = jnp.pad(cn_row.reshape(bd, t_dec * N_HEADS, t_dec), ((0, 0), (0, 0), (0, LANES - t_dec)))
        a_p, a_s = _attention(l, page_table, qk_p[0], qk_p[1], qk_p[4], z_p, bp, seq, qbd, kt_pages, vt_pages,
                              lf_pages, knew, vnew, cn_col, cn_row, z_s)

        yp, cache_p = _layer_back(l, yp, z_p, logf_p, qk_p, a_p, mods_p, h_zero, seq, bp, wl_p, 1024, 64, CHUNK, True)
        h0 = jnp.pad(_state_to_lanes(state_ssm_re[:, l], state_ssm_im[:, l]), ((0, SUBLANES - bd), (0, 0)))
        ys, cache_s = _layer_back(l, ys, z_s, logf_s, qk_s, a_s, mods_s, h0, t_dec, bd, wl_s, bd * t_dec, t_dec,
                                  bd * t_dec, False)
        outs_p.append(cache_p)
        outs_s.append(cache_s)

    def stack(outs, idx, shape):
        return jnp.stack([o[idx].reshape(shape) for o in outs], axis=1)

    knt, vt32, lft = caches_p
    k_prompt = knt.reshape(bp, depth, N_HEADS, HEAD_DIM, seq).transpose(0, 1, 4, 2, 3)
    v_prompt = vt32.reshape(bp, depth, N_HEADS, HEAD_DIM, seq).transpose(0, 1, 4, 2, 3)
    logf_prompt = lft.transpose(0, 1, 3, 2)
    return (yp.reshape(bp, seq, D_MODEL), ys.reshape(bd, t_dec, D_MODEL),
            k_prompt, v_prompt, logf_prompt,
            stack(outs_p, 0, (bp, N_SSM_GROUPS, SSM_STATE)), stack(outs_p, 1, (bp, N_SSM_GROUPS, SSM_STATE)),
            stack(outs_s, 0, (bd, t_dec, N_HEADS, HEAD_DIM)), stack(outs_s, 1, (bd, t_dec, N_HEADS, HEAD_DIM)),
            stack(outs_s, 2, (bd, t_dec, N_HEADS)),
            stack(outs_s, 3, (bd, N_SSM_GROUPS, SSM_STATE)), stack(outs_s, 4, (bd, N_SSM_GROUPS, SSM_STATE)),
            stack(outs_s, 5, (bd, t_dec, D_CMLP)))
```

```python
import functools

import jax
import jax.numpy as jnp
from jax import lax
from jax.experimental import pallas as pl
from jax.experimental.pallas import tpu as pltpu

F32 = jnp.float32
BF16 = jnp.bfloat16

D_MODEL = 2048
HEAD_DIM = 64
D_ATTN = 1024
N_HEADS = 16
D_SSM = 512
SSM_GROUP = 16
N_SSM_GROUPS = 32
SSM_STATE = 64
D_CMLP = 512
CHUNK = 128
N_CMLP_GROUPS = 4
CMLP_GROUP = 128
PAGE_SIZE = 128
EPS = 1e-6
LANES = 128
SUBLANES = 8
NEG_BIG = -1e30
LOG2E = 1.4426950408889634

COL_Q, COL_K, COL_V, COL_GA = 0, 1024, 2048, 3072
COL_XS, COL_GS, COL_U, COL_VV, COL_GM = 4096, 4608, 5120, 5632, 6144
D_Z = 6656
D_STATE = 2 * N_SSM_GROUPS * SSM_STATE
S5_SETS = 2
SET_X = D_SSM // S5_SETS
SET_H = D_STATE // S5_SETS
PAGES_PER_STEP = 8
TQ = 512
INPROJ_TN = 1664


def _cparams(sem, vmem_mb):
    return pltpu.CompilerParams(dimension_semantics=sem, vmem_limit_bytes=vmem_mb * 1024 * 1024)


def _split3(x):
    hi = x.astype(BF16)
    r1 = x - hi.astype(F32)
    mid = r1.astype(BF16)
    lo = (r1 - mid.astype(F32)).astype(BF16)
    return hi, mid, lo


def _dot(a, b):
    return jnp.dot(a, b, preferred_element_type=F32)


def _dot_nt(a, b):
    return lax.dot_general(a, b, (((1,), (1,)), ((), ())), preferred_element_type=F32)


def _silu(x):
    return x * jax.nn.sigmoid(x)


def _ada_kernel(c_ref, w_ref, b_ref, o_ref):
    s = _silu(c_ref[...]).astype(BF16)
    o_ref[...] = _dot(s, w_ref[...].astype(BF16)) + b_ref[...]


def _ada_all(c_all, w_ada, b_ada):
    depth = w_ada.shape[0]
    rows = c_all.shape[0]
    tn = 1024
    return pl.pallas_call(
        _ada_kernel,
        grid=(depth, 3 * D_MODEL // tn),
        in_specs=[
            pl.BlockSpec((rows, D_MODEL), lambda l, j: (0, 0)),
            pl.BlockSpec((None, D_MODEL, tn), lambda l, j: (l, 0, j)),
            pl.BlockSpec((None, 1, tn), lambda l, j: (l, 0, j)),
        ],
        out_specs=pl.BlockSpec((None, rows, tn), lambda l, j: (l, 0, j)),
        out_shape=jax.ShapeDtypeStruct((depth, rows, 3 * D_MODEL), F32),
        compiler_params=_cparams(("arbitrary", "arbitrary"), 48),
        name="ada",
    )(c_all, w_ada, b_ada.reshape(depth, 1, 3 * D_MODEL))


def _inproj_kernel(x_ref, g_ref, sc_ref, sh_ref, wt_ref, wfgt_ref, bf_ref, z_ref, lf_ref, h_ref):
    @pl.when(pl.program_id(1) == 0)
    def _():
        x = x_ref[...]
        ms = jnp.mean(x * x, axis=-1, keepdims=True)
        y = x * lax.rsqrt(ms + EPS) * g_ref[...]
        hb = (y * (1.0 + sc_ref[...]) + sh_ref[...]).astype(BF16)
        h_ref[...] = hb
        fg = _dot_nt(hb, wfgt_ref[...]) + bf_ref[...]
        lf_ref[...] = jnp.minimum(fg, 0.0) - jnp.log1p(jnp.exp(-jnp.abs(fg)))

    z_ref[...] = _dot_nt(h_ref[...], wt_ref[...])


def _inproj(layer, x, norm_g, scale, shift, wt_main, wt_fg, b_f, tm, rows_per_mod):
    t = x.shape[0]
    tn = INPROJ_TN
    mod_rows = scale.shape[1]
    mod_spec = pl.BlockSpec((None, mod_rows, D_MODEL), lambda i, j: ((i * tm) // rows_per_mod, 0, 0))
    return pl.pallas_call(
        _inproj_kernel,
        grid=(t // tm, D_Z // tn),
        in_specs=[
            pl.BlockSpec((tm, D_MODEL), lambda i, j: (i, 0)),
            pl.BlockSpec((1, D_MODEL), lambda i, j: (0, 0)),
            mod_spec,
            mod_spec,
            pl.BlockSpec((None, tn, D_MODEL), lambda i, j: (layer, j, 0)),
            pl.BlockSpec((None, LANES, D_MODEL), lambda i, j: (layer, 0, 0)),
            pl.BlockSpec((1, LANES), lambda i, j: (0, 0)),
        ],
        out_specs=[
            pl.BlockSpec((tm, tn), lambda i, j: (i, j)),
            pl.BlockSpec((tm, LANES), lambda i, j: (i, 0)),
        ],
        out_shape=[jax.ShapeDtypeStruct((t, D_Z), F32), jax.ShapeDtypeStruct((t, LANES), F32)],
        scratch_shapes=[pltpu.VMEM((tm, D_MODEL), BF16)],
        compiler_params=_cparams(("arbitrary", "arbitrary"), 60),
        name="inproj",
    )(x, norm_g, scale, shift, wt_main, wt_fg, b_f)


def _qkprep_kernel(seq_len, prompt, n_alias, q_ref, k_ref, v_ref, lf_ref, qg_ref, kg_ref, seg_ref, pq_ref, pk_ref,
                   *rest):
    qa_ref, ka_ref = rest[n_alias:n_alias + 2]
    rest = rest[n_alias + 2:]
    if prompt:
        knt_ref, vt32_ref, vt16_ref, lft_ref, f_ref, carry_ref = rest
    else:
        kn_ref, vb_ref, f_ref, carry_ref = rest
    tm = q_ref.shape[0]
    pairs = N_HEADS // 2
    row = lax.broadcasted_iota(jnp.int32, (tm, tm), 0)
    col = lax.broadcasted_iota(jnp.int32, (tm, tm), 1)
    tri = col <= row
    if seq_len < tm:
        shift = seq_len.bit_length() - 1
        tri = tri & (lax.shift_right_logical(row, shift) == lax.shift_right_logical(col, shift))
    tri_b = jnp.where(tri, 1.0, 0.0).astype(BF16)
    lf = lf_ref[...]
    hi, mid, lo = _split3(lf)
    f = _dot(tri_b, hi) + _dot(tri_b, mid) + _dot(tri_b, lo)
    if seq_len > tm:
        @pl.when(pl.program_id(0) % (seq_len // tm) == 0)
        def _():
            carry_ref[...] = jnp.zeros_like(carry_ref)

        f = f + carry_ref[...]
        carry_ref[...] = f[tm - 1:tm, :]
    f_ref[...] = f

    def head_norm(x, g):
        ss = _dot((x * x).astype(BF16), seg_ref[...])
        return x * lax.rsqrt(ss * (1.0 / HEAD_DIM) + EPS) * g

    qs = head_norm(q_ref[...], qg_ref[...]) * (HEAD_DIM ** -0.5 * LOG2E)
    kn = head_norm(k_ref[...], kg_ref[...])
    v = v_ref[...]
    if prompt:
        lft_ref[...] = lf.T[0:N_HEADS, :]
        for p in range(pairs):
            cols = slice(p * LANES, (p + 1) * LANES)
            knt_ref[cols, :] = kn[:, cols].T
            vt = v[:, cols].T
            vt32_ref[cols, :] = vt
            vt16_ref[p] = vt.astype(BF16)
    else:
        kn_ref[...] = kn
        vb_ref[...] = v.astype(BF16)

    fh, fm, fl = _split3(f * LOG2E)
    lane = lax.broadcasted_iota(jnp.int32, (tm, LANES), 1)
    parts = jnp.where(lane < 16, fh.astype(F32),
                      jnp.where(lane < 32, pltpu.roll(fm.astype(F32), 16, 1),
                                jnp.where(lane < 48, pltpu.roll(fl.astype(F32), 32, 1),
                                          jnp.where(lane == 48, 1.0, 0.0)))).astype(BF16)
    ext_q = _dot(parts, pq_ref[...])
    ext_k = _dot(parts, pk_ref[...])
    low = lane < HEAD_DIM
    for src, ext, dst in ((qs, ext_q, qa_ref), (kn, ext_k, ka_ref)):
        for p in range(pairs):
            blk = src[:, p * LANES:(p + 1) * LANES]
            e = ext[:, p * LANES:(p + 1) * LANES]
            dst[:, (2 * p) * LANES:(2 * p + 1) * LANES] = jnp.where(low, blk, pltpu.roll(e, 64, 1)).astype(BF16)
            dst[:, (2 * p + 1) * LANES:(2 * p + 2) * LANES] = jnp.where(low, pltpu.roll(blk, 64, 1), e).astype(BF16)


def _aug_placement():
    r = jnp.arange(LANES)[:, None]
    c = jnp.arange(N_HEADS * HEAD_DIM)[None, :]
    rp, rh = r // 16, r % 16
    ch, cc = c // HEAD_DIM, c % HEAD_DIM
    f_rows = (r < 48) & (rh == ch)
    one_row = r == 48
    pq = jnp.where(f_rows & (cc == rp), 1.0, 0.0) + jnp.where(one_row & (cc >= 3) & (cc < 6), 1.0, 0.0)
    pk = jnp.where(f_rows & (cc == rp + 3), -1.0, 0.0) + jnp.where(one_row & (cc < 3), 1.0, 0.0)
    seg = jnp.where(jnp.arange(D_ATTN)[:, None] // HEAD_DIM == jnp.arange(D_ATTN)[None, :] // HEAD_DIM, 1.0, 0.0)
    return pq.astype(BF16), pk.astype(BF16), seg.astype(BF16)


def _qkprep(z, logf, qg, kg, consts, tm, seq_len, prompt, layer=0, depth=1, caches=None):
    t = z.shape[0]
    pq, pk, seg = consts
    col = lambda c: pl.BlockSpec((tm, D_ATTN), lambda i: (i, c))
    const = lambda shape: pl.BlockSpec(shape, lambda i: (0, 0))
    rows = lambda width: pl.BlockSpec((tm, width), lambda i: (i, 0))
    pairs = N_HEADS // 2
    out_specs = [rows(2 * D_ATTN), rows(2 * D_ATTN)]
    out_shape = [jax.ShapeDtypeStruct((t, 2 * D_ATTN), BF16), jax.ShapeDtypeStruct((t, 2 * D_ATTN), BF16)]
    if prompt:
        nk = seq_len // tm
        batch = t // seq_len
        tspec = lambda width: pl.BlockSpec((None, None, width, tm), lambda i: (i // nk, layer, 0, i % nk))
        out_specs += [tspec(D_ATTN), tspec(D_ATTN),
                      pl.BlockSpec((None, pairs, None, LANES, tm), lambda i: (i // nk, 0, i % nk, 0, 0)),
                      tspec(N_HEADS)]
        out_shape += [jax.ShapeDtypeStruct((batch, depth, D_ATTN, seq_len), F32),
                      jax.ShapeDtypeStruct((batch, depth, D_ATTN, seq_len), F32),
                      jax.ShapeDtypeStruct((batch, pairs, nk, LANES, tm), BF16),
                      jax.ShapeDtypeStruct((batch, depth, N_HEADS, seq_len), F32)]
    else:
        out_specs += [rows(D_ATTN), rows(D_ATTN)]
        out_shape += [jax.ShapeDtypeStruct((t, D_ATTN), F32), jax.ShapeDtypeStruct((t, D_ATTN), BF16)]
    out_specs.append(rows(LANES))
    out_shape.append(jax.ShapeDtypeStruct((t, LANES), F32))
    in_specs = [col(COL_Q // D_ATTN), col(COL_K // D_ATTN), col(COL_V // D_ATTN), rows(LANES),
                const((1, D_ATTN)), const((1, D_ATTN)), const((D_ATTN, D_ATTN)),
                const((LANES, D_ATTN)), const((LANES, D_ATTN))]
    caches = tuple(caches or ())
    aliases = {}
    if caches:
        aliases = {len(in_specs): 2, len(in_specs) + 1: 3, len(in_specs) + 2: 5}
        in_specs = in_specs + [pl.BlockSpec(memory_space=pl.ANY)] * len(caches)
    return pl.pallas_call(
        functools.partial(_qkprep_kernel, seq_len, prompt, len(caches)),
        grid=(t // tm,),
        in_specs=in_specs,
        out_specs=out_specs,
        out_shape=out_shape,
        scratch_shapes=[pltpu.VMEM((1, LANES), F32)],
        input_output_aliases=aliases,
        compiler_params=_cparams(("arbitrary",), 56),
        name="qkprep",
    )(z, z, z, logf, qg, kg, seg, pq, pk, *caches)


def _fox_step(i, qa_ref, ka_ref, vt_ref, ga_ref, o_ref, m_ref, l_ref, acc_ref):
    tq = qa_ref.shape[0]
    tk = vt_ref.shape[2]
    m_ref[...] = jnp.full_like(m_ref, NEG_BIG)
    l_ref[...] = jnp.zeros_like(l_ref)
    acc_ref[...] = jnp.zeros_like(acc_ref)

    def tile(j, masked):
        k0 = pl.multiple_of(j * tk, tk)
        for hh in range(2):
            k = ka_ref[pl.ds(k0, tk), hh * LANES:(hh + 1) * LANES]
            s = _dot_nt(k, qa_ref[:, hh * LANES:(hh + 1) * LANES])
            if masked:
                row = lax.broadcasted_iota(jnp.int32, (tk, tq), 0)
                col = lax.broadcasted_iota(jnp.int32, (tk, tq), 1)
                s = jnp.where(row <= col, s, NEG_BIG)
            m_prev = m_ref[hh]
            m_new = jnp.maximum(m_prev, jnp.max(s, axis=0, keepdims=True))
            alpha = jnp.exp2(m_prev - m_new)
            p = jnp.exp2(s - m_new)
            l_ref[hh] = alpha * l_ref[hh] + jnp.sum(p, axis=0, keepdims=True)
            vt = vt_ref[j, hh * HEAD_DIM:(hh + 1) * HEAD_DIM, :]
            acc_ref[hh] = alpha * acc_ref[hh] + _dot(vt, p.astype(BF16))
            m_ref[hh] = m_new

    def body(j, carry):
        tile(j, False)
        return carry

    lax.fori_loop(0, i, body, 0)
    tile(i, True)
    o_t = jnp.concatenate([acc_ref[0] / l_ref[0], acc_ref[1] / l_ref[1]], axis=0)
    o_ref[...] = (o_t.T * _silu(ga_ref[...])).astype(BF16)


def _foxdec_step(first, last, q_ref, kt_refs, vt_refs, lf_refs, knew_ref, vnew_ref, cncol_ref, cnrow_ref, ga_ref,
                 o_ref, m_ref, l_ref, acc_ref, carry_ref):
    n = len(kt_refs)
    rows = q_ref.shape[0]
    npos = n * PAGE_SIZE

    @pl.when(first)
    def _():
        m_ref[...] = jnp.full_like(m_ref, NEG_BIG)
        l_ref[...] = jnp.zeros_like(l_ref)
        acc_ref[...] = jnp.zeros_like(acc_ref)
        carry_ref[...] = jnp.zeros_like(carry_ref)

    def accumulate(s, pv):
        m_prev = m_ref[...]
        m_new = jnp.maximum(m_prev, jnp.max(s, axis=1, keepdims=True))
        alpha = jnp.exp2(m_prev - m_new)
        p = jnp.exp2(s - m_new)
        l_ref[...] = alpha * l_ref[...] + jnp.sum(p, axis=1, keepdims=True)
        acc_ref[...] = alpha * acc_ref[...] + pv(p.astype(BF16))
        m_ref[...] = m_new

    row = lax.broadcasted_iota(jnp.int32, (PAGE_SIZE, PAGE_SIZE), 0)
    col = lax.broadcasted_iota(jnp.int32, (PAGE_SIZE, PAGE_SIZE), 1)
    later = jnp.where(row > col, 1.0, 0.0).astype(BF16)
    srev = [None] * n
    for i in reversed(range(n)):
        x = lf_refs[i][...]
        hi, mid, lo = _split3(x)
        srev[i] = _dot(hi, later) + _dot(mid, later) + _dot(lo, later) + carry_ref[...]
        carry_ref[...] = carry_ref[...] + jnp.sum(x, axis=1, keepdims=True)
    srev = jnp.concatenate(srev, axis=1) * LOG2E

    q = q_ref[...]
    cn_col = cncol_ref[...] * LOG2E
    kt = jnp.concatenate([r[...].astype(BF16) for r in kt_refs], axis=1)
    vt = jnp.concatenate([r[...].astype(BF16) for r in vt_refs], axis=1)
    s = _dot(q, kt)
    s = (s.reshape(rows // N_HEADS, N_HEADS, npos) + srev[None]).reshape(rows, npos)
    s = s + jnp.concatenate([cn_col] * n, axis=1)
    accumulate(s, lambda p: _dot_nt(p, vt))

    @pl.when(last)
    def _():
        sn = _dot_nt(q, knew_ref[...]) + (cn_col - cnrow_ref[...] * LOG2E)
        row = lax.broadcasted_iota(jnp.int32, (rows, LANES), 0)
        col = lax.broadcasted_iota(jnp.int32, (rows, LANES), 1)
        sn = jnp.where(col <= lax.shift_right_logical(row, 4), sn, NEG_BIG)
        accumulate(sn, lambda p: _dot(p, vnew_ref[...]))
        o = acc_ref[...] / l_ref[...]
        row = lax.broadcasted_iota(jnp.int32, (rows, D_ATTN), 0)
        col = lax.broadcasted_iota(jnp.int32, (rows, D_ATTN), 1)
        own = (row & (N_HEADS - 1)) == lax.shift_right_logical(col, 6)
        o = jnp.sum(jnp.where(own, o, 0.0).reshape(rows // N_HEADS, N_HEADS, D_ATTN), axis=1)
        o_ref[...] = o * _silu(ga_ref[...])


def _attn_kernel(nq, nchunk, pt_ref, *refs):
    n = PAGES_PER_STEP
    qa_ref, ka_ref, vtp_ref, gap_ref, q_ref = refs[:5]
    kt_refs = refs[5:5 + n]
    vt_refs = refs[5 + n:5 + 2 * n]
    lf_refs = refs[5 + 2 * n:5 + 3 * n]
    (knew_ref, vnew_ref, cncol_ref, cnrow_ref, gas_ref, op_ref, os_ref,
     mp_ref, lp_ref, accp_ref, ms_ref, ls_ref, accs_ref, carry_ref) = refs[5 + 3 * n:]
    g = pl.program_id(0)
    _fox_step(g % nq, qa_ref, ka_ref, vtp_ref, gap_ref, op_ref, mp_ref, lp_ref, accp_ref)
    c = g % nchunk
    _foxdec_step(c == 0, c == nchunk - 1, q_ref, kt_refs, vt_refs, lf_refs, knew_ref, vnew_ref, cncol_ref,
                 cnrow_ref, gas_ref, os_ref, ms_ref, ls_ref, accs_ref, carry_ref)


def _attention(layer, page_table, qa, ka, vt, z_p, batch, seq, qbd, kt_pages, vt_pages, lf_pages, knew, vnew,
               cn_col, cn_row, z_s):
    tq = vt.shape[4]
    nq = seq // tq
    pairs = N_HEADS // 2
    n_dec, n_pages = page_table.shape
    n = PAGES_PER_STEP
    nchunk = n_pages // n
    steps = batch * pairs * nq
    assert steps == n_dec * nchunk, (steps, n_dec, nchunk)
    rows = qbd.shape[1]
    t_len = rows // N_HEADS
    ga_blk = COL_GA // LANES

    def prow(g):
        return (g // (pairs * nq)) * nq + g % nq

    pcol = lambda g: (g // nq) % pairs
    drow = lambda g: g // nchunk
    dchunk = lambda g: nchunk - 1 - g % nchunk

    def page_spec(shape, i):
        return pl.BlockSpec((None, None) + shape, lambda g, pt: (pt[drow(g), dchunk(g) * n + i], layer, 0, 0))

    per_row = lambda shape: pl.BlockSpec((None,) + shape, lambda g, pt: (drow(g), 0, 0))
    grid_spec = pltpu.PrefetchScalarGridSpec(
        num_scalar_prefetch=1,
        grid=(steps,),
        in_specs=([pl.BlockSpec((tq, 2 * LANES), lambda g, pt: (prow(g), pcol(g))),
                   pl.BlockSpec((seq, 2 * LANES), lambda g, pt: (g // (pairs * nq), pcol(g))),
                   pl.BlockSpec((None, None) + vt.shape[2:], lambda g, pt: (g // (pairs * nq), pcol(g), 0, 0, 0)),
                   pl.BlockSpec((tq, LANES), lambda g, pt: (prow(g), ga_blk + pcol(g))),
                   per_row((rows, D_ATTN))]
                  + [page_spec((D_ATTN, PAGE_SIZE), i) for i in range(n)]
                  + [page_spec((D_ATTN, PAGE_SIZE), i) for i in range(n)]
                  + [page_spec((N_HEADS, PAGE_SIZE), i) for i in range(n)]
                  + [per_row((LANES, D_ATTN)), per_row((LANES, D_ATTN)),
                     per_row((rows, LANES)), per_row((rows, LANES)),
                     pl.BlockSpec((t_len, D_ATTN), lambda g, pt: (drow(g), COL_GA // D_ATTN))]),
        out_specs=[pl.BlockSpec((tq, LANES), lambda g, pt: (prow(g), pcol(g))),
                   pl.BlockSpec((t_len, D_ATTN), lambda g, pt: (drow(g), 0))],
        scratch_shapes=[pltpu.VMEM((2, 1, tq), F32), pltpu.VMEM((2, 1, tq), F32), pltpu.VMEM((2, HEAD_DIM, tq), F32),
                        pltpu.VMEM((rows, 1), F32), pltpu.VMEM((rows, 1), F32), pltpu.VMEM((rows, D_ATTN), F32),
                        pltpu.VMEM((N_HEADS, LANES), F32)],
    )
    return pl.pallas_call(
        functools.partial(_attn_kernel, nq, nchunk),
        grid_spec=grid_spec,
        out_shape=[jax.ShapeDtypeStruct((batch * seq, D_ATTN), BF16),
                   jax.ShapeDtypeStruct((n_dec * t_len, D_ATTN), F32)],
        compiler_params=_cparams(("arbitrary",), 56),
        name="attn",
    )(page_table, qa, ka, vt, z_p, qbd, *([kt_pages] * n), *([vt_pages] * n), *([lf_pages] * n),
      knew, vnew, cn_col, cn_row, z_s)


def _s5prep_kernel(lr_ref, li_ref, ldt_ref, lrx_ref, lix_ref, bre_ref, bim_ref,
                   abr_ref, abi_ref, bbr_ref, bbi_ref):
    dt = jnp.exp(ldt_ref[...])

    def discretise(lr, li):
        mag = jnp.exp(lr * dt)
        abr = mag * jnp.cos(li * dt)
        abi = mag * jnp.sin(li * dt)
        den = lr * lr + li * li
        pr = abr - 1.0
        return abr, abi, (pr * lr + abi * li) / den, (abi * lr - pr * li) / den

    abr, abi, _, _ = discretise(lr_ref[...], li_ref[...])
    abr_ref[...] = abr
    abi_ref[...] = abi
    _, _, fr, fi = discretise(lrx_ref[...], lix_ref[...])
    bre, bim = bre_ref[...], bim_ref[...]
    bbr_ref[...] = fr * bre - fi * bim
    bbi_ref[...] = fr * bim + fi * bre


def _s5prep_all(lam_re, lam_im, log_dt, b_re, b_im):
    depth, g, n = lam_re.shape
    c = b_re.shape[-1]
    small = pl.BlockSpec((None, g, n), lambda l: (l, 0, 0))
    wide = pl.BlockSpec((None, g, n * c), lambda l: (l, 0, 0))
    return pl.pallas_call(
        _s5prep_kernel,
        grid=(depth,),
        in_specs=[small, small, pl.BlockSpec((None, g, 1), lambda l: (l, 0, 0)), wide, wide, wide, wide],
        out_specs=[small, small, wide, wide],
        out_shape=[jax.ShapeDtypeStruct((depth, g, n), F32)] * 2 + [jax.ShapeDtypeStruct((depth, g, n * c), F32)] * 2,
        compiler_params=_cparams(("arbitrary",), 32),
        name="s5prep",
    )(lam_re, lam_im, log_dt.reshape(depth, g, 1),
      jnp.repeat(lam_re, c, axis=-1), jnp.repeat(lam_im, c, axis=-1),
      b_re.reshape(depth, g, n * c), b_im.reshape(depth, g, n * c))


def _s5_kernel(x_ref, h0_ref, bset_ref, cset_ref, ar_ref, ai_ref, dsk_ref, wglu_ref, bglu_ref,
               y_ref, hfin_ref, u_ref, hst_ref):
    n_seq, tt, _ = x_ref.shape
    nb = SUBLANES
    rows = tt * nb
    rows_in = n_seq * tt
    half = SET_H // 2

    @pl.when(pl.program_id(0) == 0)
    def _():
        hst_ref[...] = h0_ref[...]

    tshift = tt.bit_length() - 1
    r = lax.broadcasted_iota(jnp.int32, (rows, rows_in), 0)
    c = lax.broadcasted_iota(jnp.int32, (rows, rows_in), 1)
    to_scan = jnp.where((lax.shift_right_logical(r, 3) == (c & (tt - 1)))
                        & ((r & (nb - 1)) == lax.shift_right_logical(c, tshift)), 1.0, 0.0).astype(BF16)
    r = lax.broadcasted_iota(jnp.int32, (rows_in, rows), 0)
    c = lax.broadcasted_iota(jnp.int32, (rows_in, rows), 1)
    from_scan = jnp.where((lax.shift_right_logical(c, 3) == (r & (tt - 1)))
                          & ((c & (nb - 1)) == lax.shift_right_logical(r, tshift)), 1.0, 0.0).astype(BF16)

    def reorder(perm, v):
        hi = v.astype(BF16)
        lo = (v - hi.astype(F32)).astype(BF16)
        return _dot(perm, hi) + _dot(perm, lo)

    x = x_ref[...].reshape(rows_in, D_SSM)
    xb = _dot(to_scan, x.astype(BF16)).astype(BF16)
    for s in range(S5_SETS):
        u_ref[:, s * SET_H:(s + 1) * SET_H] = _dot(xb[:, s * SET_X:(s + 1) * SET_X], bset_ref[s])

    ar, ai = ar_ref[...], ai_ref[...]

    def step(t, h):
        r0 = pl.multiple_of(t * nb, nb)
        u = u_ref[pl.ds(r0, nb), :]
        out = []
        for s in range(S5_SETS):
            o = s * SET_H
            hr, hi = h[:, o:o + half], h[:, o + half:o + SET_H]
            a_r, a_i = ar[:, s * half:(s + 1) * half], ai[:, s * half:(s + 1) * half]
            out.append(a_r * hr - a_i * hi + u[:, o:o + half])
            out.append(a_r * hi + a_i * hr + u[:, o + half:o + SET_H])
        hn = jnp.concatenate(out, axis=1)
        u_ref[pl.ds(r0, nb), :] = hn
        return hn

    h = lax.fori_loop(0, rows // nb, step, hst_ref[...])
    hst_ref[...] = h
    hfin_ref[...] = h

    hb = u_ref[...].astype(BF16)
    y = jnp.concatenate([_dot(hb[:, s * SET_H:(s + 1) * SET_H], cset_ref[s]) for s in range(S5_SETS)], axis=1)
    y = reorder(from_scan, y) + dsk_ref[...] * x
    y = 0.5 * y * (1.0 + jnp.tanh(0.7978845608028654 * (y + 0.044715 * (y * y * y))))
    y = y * jax.nn.sigmoid(_dot(y.astype(BF16), wglu_ref[...]) + bglu_ref[...])
    y_ref[...] = y.reshape(n_seq, tt, D_SSM)


def _s5(z3, h0, bset, cset, ar, ai, d_skip, w_glu, b_glu, tt):
    n_seq, seq, _ = z3.shape
    blk = tt * SUBLANES
    const2 = lambda shape: pl.BlockSpec(shape, lambda i: (0, 0))
    const3 = lambda shape: pl.BlockSpec(shape, lambda i: (0, 0, 0))
    return pl.pallas_call(
        _s5_kernel,
        grid=(seq // tt,),
        in_specs=[pl.BlockSpec((n_seq, tt, D_SSM), lambda i: (0, i, COL_XS // D_SSM)),
                  const2((SUBLANES, D_STATE)),
                  const3((S5_SETS, SET_X, SET_H)), const3((S5_SETS, SET_H, SET_X)),
                  const2((SUBLANES, D_STATE // 2)), const2((SUBLANES, D_STATE // 2)),
                  const2((1, D_SSM)), const2((D_SSM, D_SSM)), const2((1, D_SSM))],
        out_specs=[pl.BlockSpec((n_seq, tt, D_SSM), lambda i: (0, i, 0)), const2((SUBLANES, D_STATE))],
        out_shape=[jax.ShapeDtypeStruct((n_seq, seq, D_SSM), F32), jax.ShapeDtypeStruct((SUBLANES, D_STATE), F32)],
        scratch_shapes=[pltpu.VMEM((blk, D_STATE), F32), pltpu.VMEM((SUBLANES, D_STATE), F32)],
        compiler_params=_cparams(("arbitrary",), 48),
        name="s5",
    )(z3, h0, bset, cset, ar, ai, d_skip, w_glu, b_glu)


def _s5_operands(abr, abi, bbr, bbi, c_re, c_im):
    g, n, c = N_SSM_GROUPS, SSM_STATE, SSM_GROUP
    gs = g // S5_SETS
    eye = jnp.eye(gs, dtype=F32)

    def in_map(bb):
        bb = bb.reshape(S5_SETS, gs, n, c).transpose(0, 1, 3, 2)
        return (bb[:, :, :, None, :] * eye[None, :, None, :, None]).reshape(S5_SETS, gs * c, gs * n)

    def out_map(cc):
        cc = cc.reshape(S5_SETS, gs, c, n).transpose(0, 1, 3, 2)
        return (cc[:, :, :, None, :] * eye[None, :, None, :, None]).reshape(S5_SETS, gs * n, gs * c)

    bset = jnp.concatenate([in_map(bbr), in_map(bbi)], axis=2).astype(BF16)
    cset = jnp.concatenate([out_map(c_re), -out_map(c_im)], axis=1).astype(BF16)
    ar = jnp.broadcast_to(abr.reshape(1, g * n), (SUBLANES, g * n))
    ai = jnp.broadcast_to(abi.reshape(1, g * n), (SUBLANES, g * n))
    return bset, cset, ar, ai


def _state_to_lanes(h_re, h_im):
    nb = h_re.shape[0]
    gs = N_SSM_GROUPS // S5_SETS
    st = jnp.stack([h_re.reshape(nb, S5_SETS, gs, SSM_STATE), h_im.reshape(nb, S5_SETS, gs, SSM_STATE)], axis=2)
    return st.reshape(nb, D_STATE)


def _lanes_to_state(h):
    nb = h.shape[0]
    gs = N_SSM_GROUPS // S5_SETS
    st = h.reshape(nb, S5_SETS, 2, gs, SSM_STATE)
    return (st[:, :, 0].reshape(nb, N_SSM_GROUPS, SSM_STATE), st[:, :, 1].reshape(nb, N_SSM_GROUPS, SSM_STATE))


def _cmlp_kernel(lc, emit_vn, u_ref, v_ref, gm_ref, g_ref, b_ref, w_ref, bias_ref, o_ref, *vn_ref):
    tm = u_ref.shape[0]
    v = v_ref[...]
    mu = jnp.mean(v, axis=-1, keepdims=True)
    d = v - mu
    var = jnp.mean(d * d, axis=-1, keepdims=True)
    vn = d * lax.rsqrt(var + EPS) * g_ref[...] + b_ref[...]
    if emit_vn:
        vn_ref[0][...] = vn
    row = lax.broadcasted_iota(jnp.int32, (lc, lc), 0)
    col = lax.broadcasted_iota(jnp.int32, (lc, lc), 1)
    ug = u_ref[...] * _silu(gm_ref[...])
    vb = vn.astype(BF16)
    for g in range(N_CMLP_GROUPS):
        wg = jnp.where(col <= row, w_ref[g], 0.0).astype(BF16)
        lanes = slice(g * CMLP_GROUP, (g + 1) * CMLP_GROUP)
        for c in range(tm // lc):
            rws = slice(c * lc, (c + 1) * lc)
            zc = _dot(wg, vb[rws, lanes]) + bias_ref[:, lanes]
            o_ref[rws, lanes] = (ug[rws, lanes] * zc).astype(BF16)


def _cmlp(z, ln_g, ln_b, w, bias, tm, lc, emit_vn):
    t = z.shape[0]
    col = lambda c: pl.BlockSpec((tm, D_CMLP), lambda i: (i, c))
    const2 = lambda shape: pl.BlockSpec(shape, lambda i: (0, 0))
    out_specs = [pl.BlockSpec((tm, D_CMLP), lambda i: (i, 0))]
    out_shape = [jax.ShapeDtypeStruct((t, D_CMLP), BF16)]
    if emit_vn:
        out_specs.append(pl.BlockSpec((tm, D_CMLP), lambda i: (i, 0)))
        out_shape.append(jax.ShapeDtypeStruct((t, D_CMLP), F32))
    return pl.pallas_call(
        functools.partial(_cmlp_kernel, lc, emit_vn),
        grid=(t // tm,),
        in_specs=[col(COL_U // D_CMLP), col(COL_VV // D_CMLP), col(COL_GM // D_CMLP),
                  const2((1, D_CMLP)), const2((1, D_CMLP)),
                  pl.BlockSpec((N_CMLP_GROUPS, lc, lc), lambda i: (0, 0, 0)),
                  const2((lc, D_CMLP))],
        out_specs=out_specs,
        out_shape=out_shape,
        compiler_params=_cparams(("arbitrary",), 32),
        name="cmlp",
    )(z, z, z, ln_g, ln_b, w, bias)


def _outproj_kernel(x_ref, a_ref, s_ref, gs_ref, m_ref, gate_ref, w_ref, o_ref):
    sg = (s_ref[...] * _silu(gs_ref[...])).astype(BF16)
    acc = (_dot(a_ref[...].astype(BF16), w_ref[0:D_ATTN, :])
           + _dot(sg, w_ref[D_ATTN:D_ATTN + D_SSM, :])
           + _dot(m_ref[...], w_ref[D_ATTN + D_SSM:, :]))
    o_ref[...] = x_ref[...] + gate_ref[...] * acc


def _outproj(layer, x, a, s, z, m, gate, w_out, tm, rows_per_mod):
    t = x.shape[0]
    mod_rows = gate.shape[1]
    return pl.pallas_call(
        _outproj_kernel,
        grid=(t // tm,),
        in_specs=[pl.BlockSpec((tm, D_MODEL), lambda i: (i, 0)),
                  pl.BlockSpec((tm, D_ATTN), lambda i: (i, 0)),
                  pl.BlockSpec((tm, D_SSM), lambda i: (i, 0)),
                  pl.BlockSpec((tm, D_SSM), lambda i: (i, COL_GS // D_SSM)),
                  pl.BlockSpec((tm, D_CMLP), lambda i: (i, 0)),
                  pl.BlockSpec((None, mod_rows, D_MODEL), lambda i: ((i * tm) // rows_per_mod, 0, 0)),
                  pl.BlockSpec((None, D_MODEL, D_MODEL), lambda i: (layer, 0, 0))],
        out_specs=pl.BlockSpec((tm, D_MODEL), lambda i: (i, 0)),
        out_shape=jax.ShapeDtypeStruct((t, D_MODEL), F32),
        compiler_params=_cparams(("arbitrary",), 56),
        name="outproj",
    )(x, a, s, z, m, gate, w_out)


def _pack_w_in(w_in):
    d_q = 3 * D_ATTN
    wt = w_in.transpose(0, 2, 1)
    main = jnp.concatenate([wt[:, :d_q], wt[:, d_q + N_HEADS:]], axis=1).astype(BF16)
    wt_fg = jnp.pad(wt[:, d_q:d_q + N_HEADS], ((0, 0), (0, LANES - N_HEADS), (0, 0))).astype(BF16)
    return main, wt_fg


def _layer_front(layer, depth, x, mods, wl, consts, tm, seq_len, prompt, caches=None):
    shift, scale, _, rows_per_mod = mods
    z, logf = _inproj(layer, x, wl["norm_g"], scale, shift, wl["wt_main"], wl["wt_fg"], wl["b_f"], tm, rows_per_mod)
    qk = _qkprep(z, logf, wl["qg"], wl["kg"], consts, min(tm, TQ), seq_len, prompt, layer, depth, caches)
    return z, logf, qk


def _layer_back(layer, x, z, logf, qk, a_out, mods, h0_lanes, seq_len, batch, wl, tm, tt, lc, prompt):
    t = x.shape[0]
    gate, rows_per_mod = mods[2], mods[3]
    s_pre, hfin = _s5(z.reshape(batch, seq_len, D_Z), h0_lanes, wl["bset"], wl["cset"], wl["ar"], wl["ai"],
                      wl["d_skip"], wl["w_glu"], wl["b_glu"], tt)
    s_pre = s_pre.reshape(t, D_SSM)

    m_res = _cmlp(z, wl["ln_g"], wl["ln_b"], wl["w_s"], wl["b_s"], min(tm, 512), lc, not prompt)
    y = _outproj(layer, x, a_out, s_pre, z, m_res[0], gate, wl["w_out"], min(tm, 512), rows_per_mod)
    h_re, h_im = _lanes_to_state(hfin[:batch])
    if prompt:
        return y, (h_re, h_im)
    _, _, kn, _, _ = qk
    return y, (kn, z[:, COL_V:COL_V + D_ATTN], logf[:, :N_HEADS], h_re, h_im, m_res[1])


def kernel(x_prompt, x_sample, c_prompt, c_sample, cache_k, cache_v, cache_logf, state_ssm_re, state_ssm_im,
           page_table, norm_g, w_ada, b_ada, w_in, b_f, q_norm_g, k_norm_g, lam_re, lam_im, log_dt,
           b_re, b_im, c_re, c_im, d_skip, w_glu, b_glu, sgu_ln_g, sgu_ln_b, w_s, b_s, w_out):
    depth = w_in.shape[0]
    bp, seq, _ = x_prompt.shape
    bd, t_dec, _ = x_sample.shape
    n_pool = cache_k.shape[0]

    c_all = jnp.concatenate([c_prompt, c_sample], axis=0)
    c_all = jnp.pad(c_all, ((0, 2 * SUBLANES - bp - bd), (0, 0)))
    mod = _ada_all(c_all, w_ada, b_ada)

    wt_main, wt_fg = _pack_w_in(w_in)
    w_out_b = w_out.astype(BF16)
    w_glu_b = w_glu.astype(BF16)
    b_f_pad = jnp.pad(b_f, ((0, 0), (0, LANES - N_HEADS))).reshape(depth, 1, LANES)
    consts = _aug_placement()
    abr, abi, bbr, bbi = _s5prep_all(lam_re, lam_im, log_dt, b_re, b_im)

    kt_pages = cache_k.transpose(0, 1, 3, 4, 2).reshape(n_pool, depth, D_ATTN, PAGE_SIZE)
    vt_pages = cache_v.transpose(0, 1, 3, 4, 2).reshape(n_pool, depth, D_ATTN, PAGE_SIZE)
    lf_pages = cache_logf.transpose(0, 1, 3, 2)

    eye_d = jnp.eye(bd, dtype=F32)
    w_s_dec = (eye_d[None, None, :, None, :, None] * w_s[:, :, None, :t_dec, None, :t_dec]).reshape(
        depth, N_CMLP_GROUPS, bd * t_dec, bd * t_dec)
    bias_p = jnp.repeat(b_s.transpose(0, 2, 1), CMLP_GROUP, axis=-1)
    bias_d = jnp.tile(bias_p[:, :t_dec], (1, bd, 1))

    yp = x_prompt.reshape(bp * seq, D_MODEL)
    ys = x_sample.reshape(bd * t_dec, D_MODEL)
    h_zero = jnp.zeros((SUBLANES, D_STATE), F32)
    eye_h = jnp.eye(N_HEADS, dtype=BF16)
    outs_p, outs_s = [], []
    caches_p = None
    for l in range(depth):
        bset, cset, ar, ai = _s5_operands(abr[l], abi[l], bbr[l], bbi[l], c_re[l], c_im[l])
        wl = dict(norm_g=norm_g[l][None], wt_main=wt_main, wt_fg=wt_fg, b_f=b_f_pad[l],
                  qg=jnp.tile(q_norm_g[l], N_HEADS)[None], kg=jnp.tile(k_norm_g[l], N_HEADS)[None],
                  bset=bset, cset=cset, ar=ar, ai=ai, d_skip=d_skip[l][None], w_glu=w_glu_b[l],
                  b_glu=b_glu[l][None], ln_g=sgu_ln_g[l][None], ln_b=sgu_ln_b[l][None], w_out=w_out_b)
        wl_p = dict(wl, w_s=w_s[l], b_s=bias_p[l])
        wl_s = dict(wl, w_s=w_s_dec[l], b_s=bias_d[l])
        shift, scale, gate = (mod[l][:, i * D_MODEL:(i + 1) * D_MODEL] for i in range(3))
        mods_p = (shift[:bp, None], scale[:bp, None], gate[:bp, None], seq)
        rep = lambda m: jnp.repeat(m[bp:bp + bd], t_dec, axis=0)[None]
        mods_s = (rep(shift), rep(scale), rep(gate), bd * t_dec)

        z_p, logf_p, qk_p = _layer_front(l, depth, yp, mods_p, wl_p, consts, 1024, seq, True, caches_p)
        caches_p = (qk_p[2], qk_p[3], qk_p[5])
        z_s, logf_s, qk_s = _layer_front(l, depth, ys, mods_s, wl_s, consts, bd * t_dec, t_dec, False)

        qa, ka, _, vb, f = qk_s
        q = qa.reshape(bd, t_dec, N_HEADS, 2 * HEAD_DIM)[..., :HEAD_DIM]
        qbd = (q[:, :, :, None, :] * eye_h[None, None, :, :, None]).reshape(bd, t_dec * N_HEADS, D_ATTN)
        kb = ka.reshape(bd, t_dec, N_HEADS, 2 * HEAD_DIM)[..., :HEAD_DIM].reshape(bd, t_dec, D_ATTN)
        knew = jnp.pad(kb, ((0, 0), (0, LANES - t_dec), (0, 0)))
        vnew = jnp.pad(vb.reshape(bd, t_dec, D_ATTN), ((0, 0), (0, LANES - t_dec), (0, 0)))
        cn = f[:, :N_HEADS].reshape(bd, t_dec, N_HEADS)
        cn_col = jnp.broadcast_to(cn.reshape(bd, t_dec * N_HEADS, 1), (bd, t_dec * N_HEADS, LANES))
        cn_row = jnp.broadcast_to(cn.transpose(0, 2, 1)[:, None], (bd, t_dec, N_HEADS, t_dec))
        cn_row = jnp.pad(cn_row.reshape(bd, t_dec * N_HEADS, t_dec), ((0, 0), (0, 0), (0, LANES - t_dec)))
        a_p, a_s = _attention(l, page_table, qk_p[0], qk_p[1], qk_p[4], z_p, bp, seq, qbd, kt_pages, vt_pages,
                              lf_pages, knew, vnew, cn_col, cn_row, z_s)

        yp, cache_p = _layer_back(l, yp, z_p, logf_p, qk_p, a_p, mods_p, h_zero, seq, bp, wl_p, 1024, 64, CHUNK, True)
        h0 = jnp.pad(_state_to_lanes(state_ssm_re[:, l], state_ssm_im[:, l]), ((0, SUBLANES - bd), (0, 0)))
        ys, cache_s = _layer_back(l, ys, z_s, logf_s, qk_s, a_s, mods_s, h0, t_dec, bd, wl_s, bd * t_dec, t_dec,
                                  bd * t_dec, False)
        outs_p.append(cache_p)
        outs_s.append(cache_s)

    def stack(outs, idx, shape):
        return jnp.stack([o[idx].reshape(shape) for o in outs], axis=1)

    knt, vt32, lft = caches_p
    k_prompt = knt.reshape(bp, depth, N_HEADS, HEAD_DIM, seq).transpose(0, 1, 4, 2, 3)
    v_prompt = vt32.reshape(bp, depth, N_HEADS, HEAD_DIM, seq).transpose(0, 1, 4, 2, 3)
    logf_prompt = lft.transpose(0, 1, 3, 2)
    return (yp.reshape(bp, seq, D_MODEL), ys.reshape(bd, t_dec, D_MODEL),
            k_prompt, v_prompt, logf_prompt,
            stack(outs_p, 0, (bp, N_SSM_GROUPS, SSM_STATE)), stack(outs_p, 1, (bp, N_SSM_GROUPS, SSM_STATE)),
            stack(outs_s, 0, (bd, t_dec, N_HEADS, HEAD_DIM)), stack(outs_s, 1, (bd, t_dec, N_HEADS, HEAD_DIM)),
            stack(outs_s, 2, (bd, t_dec, N_HEADS)),
            stack(outs_s, 3, (bd, N_SSM_GROUPS, SSM_STATE)), stack(outs_s, 4, (bd, N_SSM_GROUPS, SSM_STATE)),
            stack(outs_s, 5, (bd, t_dec, D_CMLP)))
```
